```python
import jax, jax.numpy as jnp
from jax import lax
import numpy as np

D_MODEL = 1024
BATCH = 16
SEQ = 4096
DEPTH = 1

CHUNK = 64
M_HEADS = 4
M_HEAD_DIM = 256
M_WIDTH = M_HEADS * M_HEAD_DIM
SB_HEADS = 8
SB_HEAD_DIM = 128
SB_WIDTH = SB_HEADS * SB_HEAD_DIM
CONV_WIDTH = 4
Q_BLOCK = 128
N_BRANCH = 2
EPS = 1e-6
SPLIT_SIZES = (M_WIDTH, M_WIDTH, M_WIDTH,
               M_HEADS, M_HEADS,
               M_WIDTH, M_WIDTH,
               SB_WIDTH, SB_WIDTH, SB_WIDTH,
               SB_WIDTH,
               N_BRANCH * D_MODEL)
IN_COLS = 5 * M_WIDTH + 2 * M_HEADS + 4 * SB_WIDTH + N_BRANCH * D_MODEL

kernel_name = "hybrid_mlstm_stickbreaking_gated_merge"


def rmsnorm(x, w):
    xf = x.astype(jnp.float32)
    y = xf * lax.rsqrt(jnp.mean(xf * xf, axis=-1, keepdims=True) + EPS)
    return (y * w.astype(jnp.float32)).astype(x.dtype)


def causal_depthwise_conv(x, w, b):
    k = w.shape[0]
    out = lax.conv_general_dilated(
        x, w[:, None, :].astype(x.dtype), window_strides=(1,), padding=[(k - 1, 0)],
        dimension_numbers=("NWC", "WIO", "NWC"), feature_group_count=x.shape[-1])
    return out + b


def to_heads(x, n_heads):
    b, s, _ = x.shape
    return x.reshape(b, s, n_heads, -1).transpose(0, 2, 1, 3)


def mlstm_chunkwise(q, k, v, log_i, log_f):
    b_, h_, s_, d_ = q.shape
    nc = s_ // CHUNK
    f32 = jnp.float32

    def chunked(a):
        a = a.reshape(b_, h_, nc, CHUNK, *a.shape[3:])
        return jnp.moveaxis(a, 2, 0)

    qc = chunked(q.astype(f32))
    kc = chunked(k.astype(f32) * (d_ ** -0.5))
    vc = chunked(v.astype(f32))
    ic = chunked(log_i.astype(f32))
    fc = chunked(log_f.astype(f32))
    tril = jnp.tril(jnp.ones((CHUNK, CHUNK), dtype=bool))

    def step(carry, xs):
        C, n, m = carry
        qb, kb, vb, ib, fb = xs
        bcum = jnp.cumsum(fb, axis=-1)
        dmat = bcum[..., :, None] - bcum[..., None, :] + ib[..., None, :]
        dmat = jnp.where(tril, dmat, -jnp.inf)
        inter = bcum + m[..., None]
        m_t = jnp.maximum(inter, jnp.max(dmat, axis=-1))
        w_intra = jnp.exp(dmat - m_t[..., None])
        w_inter = jnp.exp(inter - m_t)
        scores = jnp.einsum('bhtd,bhsd->bhts', qb, kb) * w_intra
        num = (jnp.einsum('bhts,bhse->bhte', scores, vb)
               + w_inter[..., None] * jnp.einsum('bhtd,bhde->bhte', qb, C))
        den = jnp.sum(scores, axis=-1) + w_inter * jnp.einsum('bhtd,bhd->bht', qb, n)
        h = num / jnp.maximum(jnp.abs(den), jnp.exp(-m_t))[..., None]
        b_last = bcum[..., -1]
        decay = b_last[..., None] - bcum + ib
        m_new = jnp.maximum(b_last + m, jnp.max(decay, axis=-1))
        ws = jnp.exp(decay - m_new[..., None])
        carry_scale = jnp.exp(b_last + m - m_new)
        kw = kb * ws[..., None]
        C_new = carry_scale[..., None, None] * C + jnp.einsum('bhsd,bhse->bhde', kw, vb)
        n_new = carry_scale[..., None] * n + jnp.sum(kw, axis=2)
        return (C_new, n_new, m_new), h

    init = (jnp.zeros((b_, h_, d_, d_), f32), jnp.zeros((b_, h_, d_), f32),
            jnp.zeros((b_, h_), f32))
    _, hs = lax.scan(step, init, (qc, kc, vc, ic, fc))
    return jnp.moveaxis(hs, 0, 2).reshape(b_, h_, s_, d_)


def stick_breaking_attention(q, k, v):
    _, _, s_, d_ = q.shape
    scale = d_ ** -0.5
    outs = []
    for blk in range(s_ // Q_BLOCK):
        t0, t1 = blk * Q_BLOCK, (blk + 1) * Q_BLOCK
        qb, kb, vb = q[:, :, t0:t1], k[:, :, :t1], v[:, :, :t1]
        z = jnp.einsum('bhtd,bhsd->bhts', qb, kb).astype(jnp.float32) * scale
        t_idx = t0 + jnp.arange(Q_BLOCK)[:, None]
        s_idx = jnp.arange(t1)[None, :]
        causal = s_idx < t_idx
        log_beta = jax.nn.log_sigmoid(z)
        log_1mb = jnp.where(causal, jax.nn.log_sigmoid(-z), 0.0)
        between = lax.cumsum(log_1mb, axis=3, reverse=True) - log_1mb
        a = jnp.where(causal, jnp.exp(log_beta + between), 0.0)
        outs.append(jnp.einsum('bhts,bhsd->bhtd', a.astype(v.dtype), vb))
    return jnp.concatenate(outs, axis=2)


def setup_inputs(seed: int = 0) -> dict:
    key = jax.random.key(seed)
    ks = jax.random.split(key, 13)
    f32 = jnp.float32
    x = jax.random.normal(ks[0], (BATCH, SEQ, D_MODEL), f32)
    norm_w = 1.0 + 0.02 * jax.random.normal(ks[1], (D_MODEL,), f32)
    w_in = jax.random.normal(ks[2], (D_MODEL, IN_COLS), f32) * D_MODEL ** -0.5
    b_in = 0.02 * jax.random.normal(ks[3], (IN_COLS,), f32)
    f_start = 3 * M_WIDTH + M_HEADS
    b_in = b_in.at[f_start:f_start + M_HEADS].add(jnp.linspace(3.0, 6.0, M_HEADS, dtype=f32))
    conv_w = jax.random.normal(ks[4], (CONV_WIDTH, 2 * M_WIDTH), f32) * CONV_WIDTH ** -0.5
    conv_b = 0.02 * jax.random.normal(ks[5], (2 * M_WIDTH,), f32)
    mlstm_norm_w = 1.0 + 0.02 * jax.random.normal(ks[6], (M_WIDTH,), f32)
    sb_q_norm_w = 1.0 + 0.02 * jax.random.normal(ks[7], (SB_HEAD_DIM,), f32)
    sb_k_norm_w = 1.0 + 0.02 * jax.random.normal(ks[8], (SB_HEAD_DIM,), f32)
    w_proj_m = jax.random.normal(ks[9], (M_WIDTH, D_MODEL), f32) * M_WIDTH ** -0.5
    w_proj_s = jax.random.normal(ks[10], (SB_WIDTH, D_MODEL), f32) * SB_WIDTH ** -0.5
    w_out = jax.random.normal(ks[11], (D_MODEL, D_MODEL), f32) * D_MODEL ** -0.5
    return {"x": x, "norm_w": norm_w, "w_in": w_in, "b_in": b_in,
            "conv_w": conv_w, "conv_b": conv_b, "mlstm_norm_w": mlstm_norm_w,
            "sb_q_norm_w": sb_q_norm_w, "sb_k_norm_w": sb_k_norm_w,
            "w_proj_m": w_proj_m, "w_proj_s": w_proj_s, "w_out": w_out}


def reference(x, norm_w, w_in, b_in, conv_w, conv_b, mlstm_norm_w,
              sb_q_norm_w, sb_k_norm_w, w_proj_m, w_proj_s, w_out):
    b_, s_, _ = x.shape
    split_at = [int(c) for c in np.cumsum(SPLIT_SIZES)[:-1]]
    for _layer in range(DEPTH):
        h = rmsnorm(x, norm_w)
        proj = jnp.einsum('bsd,de->bse', h, w_in) + b_in
        (mq, mk, mv, mi, mf, mo, mz, sq, sk, sv, sz, gates) = jnp.split(proj, split_at, axis=-1)

        qk = jax.nn.silu(causal_depthwise_conv(jnp.concatenate([mq, mk], axis=-1), conv_w, conv_b))
        mq_c, mk_c = jnp.split(qk, 2, axis=-1)
        log_i = mi.astype(jnp.float32).transpose(0, 2, 1)
        log_f = jax.nn.log_sigmoid(mf.astype(jnp.float32)).transpose(0, 2, 1)
        hm = mlstm_chunkwise(to_heads(mq_c, M_HEADS), to_heads(mk_c, M_HEADS),
                             to_heads(mv, M_HEADS), log_i, log_f)
        hm = rmsnorm(hm.transpose(0, 2, 1, 3), mlstm_norm_w.reshape(M_HEADS, M_HEAD_DIM))
        y_m = (hm.reshape(b_, s_, M_WIDTH).astype(x.dtype)
               * jax.nn.sigmoid(mo) * jax.nn.silu(mz))

        qs = rmsnorm(to_heads(sq, SB_HEADS), sb_q_norm_w)
        ks_ = rmsnorm(to_heads(sk, SB_HEADS), sb_k_norm_w)
        os_ = stick_breaking_attention(qs, ks_, to_heads(sv, SB_HEADS))
        y_s = os_.transpose(0, 2, 1, 3).reshape(b_, s_, SB_WIDTH) * jax.nn.silu(sz)

        g_m, g_s = jnp.split(jax.nn.sigmoid(gates), N_BRANCH, axis=-1)
        merged = (g_m * jnp.einsum('bse,ed->bsd', y_m, w_proj_m)
                  + g_s * jnp.einsum('bse,ed->bsd', y_s, w_proj_s))
        x = x + jnp.einsum('bsd,de->bse', merged, w_out)
    return x
```

```python
import functools

import jax
import jax.numpy as jnp
from jax import lax
from jax.experimental import pallas as pl
from jax.experimental.pallas import tpu as pltpu

F32 = jnp.float32
BF16 = jnp.bfloat16

EPS = 1e-6
D_MODEL = 1024
CHUNK = 64
M_HEADS = 4
M_HEAD_DIM = 256
M_WIDTH = M_HEADS * M_HEAD_DIM
SB_HEADS = 8
SB_HEAD_DIM = 128
SB_WIDTH = SB_HEADS * SB_HEAD_DIM
CONV_WIDTH = 4
N_BRANCH = 2

SEG_MQ, SEG_MK, SEG_MV, SEG_MO, SEG_MZ, SEG_SQ, SEG_SK, SEG_SV, SEG_SZ, SEG_GM, SEG_GS = range(11)
N_SEG = 11
SEG_W = 1024
GATE_PAD = 128
GATE_ROWS = 16
CONV_HALO = 8

VMEM_LIMIT = 56 * 1024 * 1024


def _logsig(x):
    return jnp.minimum(x, 0.0) - jnp.log1p(jnp.exp(-jnp.abs(x)))


def _sigmoid(x):
    return 1.0 / (1.0 + jnp.exp(-x))


def _inproj_kernel(x_ref, nw_ref, w_ref, b_ref, wg_ref, bg_ref, wgt_ref, bgt_ref,
                   proj_ref, gcol_ref, grow_ref, h_ref):
    @pl.when(pl.program_id(1) == 0)
    def _():
        x = x_ref[...]
        y = x * lax.rsqrt(jnp.mean(x * x, axis=-1, keepdims=True) + EPS) * nw_ref[...]
        hb = y.astype(BF16)
        h_ref[...] = hb
        gcol_ref[...] = jnp.dot(hb, wg_ref[...], preferred_element_type=F32) + bg_ref[...]
        gt = lax.dot_general(wgt_ref[...], hb, (((1,), (1,)), ((), ())), preferred_element_type=F32)
        grow_ref[...] = gt[0:8, :] + bgt_ref[...]

    acc = jnp.dot(h_ref[...], w_ref[...], preferred_element_type=F32)
    proj_ref[...] = (acc + b_ref[...]).astype(BF16)


def _inproj(x2, norm_w, w_main, b_main, wg, bg, wgt, bgt, tm):
    m_rows = x2.shape[0]
    grid = (m_rows // tm, N_SEG)
    return pl.pallas_call(
        _inproj_kernel,
        grid=grid,
        in_specs=[
            pl.BlockSpec((tm, D_MODEL), lambda i, j: (i, 0)),
            pl.BlockSpec((1, D_MODEL), lambda i, j: (0, 0)),
            pl.BlockSpec((D_MODEL, SEG_W), lambda i, j: (0, j)),
            pl.BlockSpec((1, SEG_W), lambda i, j: (0, j)),
            pl.BlockSpec((D_MODEL, GATE_PAD), lambda i, j: (0, 0)),
            pl.BlockSpec((1, GATE_PAD), lambda i, j: (0, 0)),
            pl.BlockSpec((GATE_ROWS, D_MODEL), lambda i, j: (0, 0)),
            pl.BlockSpec((8, 1), lambda i, j: (0, 0)),
        ],
        out_specs=[
            pl.BlockSpec((None, tm, SEG_W), lambda i, j: (j, i, 0)),
            pl.BlockSpec((tm, GATE_PAD), lambda i, j: (i, 0)),
            pl.BlockSpec((8, tm), lambda i, j: (0, i)),
        ],
        out_shape=[
            jax.ShapeDtypeStruct((N_SEG, m_rows, SEG_W), BF16),
            jax.ShapeDtypeStruct((m_rows, GATE_PAD), F32),
            jax.ShapeDtypeStruct((8, m_rows), F32),
        ],
        scratch_shapes=[pltpu.VMEM((tm, D_MODEL), BF16)],
        compiler_params=pltpu.CompilerParams(
            dimension_semantics=("arbitrary", "arbitrary"), vmem_limit_bytes=VMEM_LIMIT),
        name="inproj",
    )(x2, norm_w, w_main, b_main, wg, bg, wgt, bgt)


def _mlstm_kernel(q_ref, k_ref, v_ref, o_ref, z_ref, gcol_ref, grow_ref,
                  cwq_ref, cwk_ref, cbq_ref, cbk_ref, nw_ref,
                  y_ref, qext, kext, hbuf, c_ref, n_ref, m_ref, *, ts):
    head = pl.program_id(1)
    t_idx = pl.program_id(2)
    hd = M_HEAD_DIM

    @pl.when(t_idx == 0)
    def _():
        qext[0:CONV_HALO, :] = jnp.zeros((CONV_HALO, hd), F32)
        kext[0:CONV_HALO, :] = jnp.zeros((CONV_HALO, hd), F32)
        c_ref[...] = jnp.zeros_like(c_ref)
        n_ref[...] = jnp.zeros_like(n_ref)
        m_ref[...] = jnp.zeros_like(m_ref)

    @pl.when(t_idx > 0)
    def _():
        qext[0:CONV_HALO, :] = qext[ts:ts + CONV_HALO, :]
        kext[0:CONV_HALO, :] = kext[ts:ts + CONV_HALO, :]

    qext[CONV_HALO:CONV_HALO + ts, :] = q_ref[...].astype(F32)
    kext[CONV_HALO:CONV_HALO + ts, :] = k_ref[...].astype(F32)

    def conv_silu(ext, cw_ref, cb_ref):
        acc = cb_ref[...]
        for j in range(CONV_WIDTH):
            off = CONV_HALO - (CONV_WIDTH - 1) + j
            acc = acc + cw_ref[j:j + 1, :] * ext[off:off + ts, :]
        return acc * _sigmoid(acc)

    qc = conv_silu(qext, cwq_ref, cbq_ref)
    kc = conv_silu(kext, cwk_ref, cbk_ref) * (hd ** -0.5)

    g = gcol_ref[...]
    lane = lax.broadcasted_iota(jnp.int32, g.shape, 1)
    icol_all = jnp.sum(jnp.where(lane == head, g, 0.0), axis=-1, keepdims=True)
    fcol_all = _logsig(jnp.sum(jnp.where(lane == head + M_HEADS, g, 0.0), axis=-1, keepdims=True))
    irow_all = grow_ref[pl.ds(head, 1), :]
    frow_all = _logsig(grow_ref[pl.ds(head + M_HEADS, 1), :])

    r_i = lax.broadcasted_iota(jnp.int32, (CHUNK, CHUNK), 0)
    c_i = lax.broadcasted_iota(jnp.int32, (CHUNK, CHUNK), 1)
    tril = r_i >= c_i

    m_prev = m_ref[:, 0:1]
    for c in range(ts // CHUNK):
        lo, hi = c * CHUNK, (c + 1) * CHUNK
        qf = qc[lo:hi, :]
        kf = kc[lo:hi, :]
        qb = qf.astype(BF16)
        kb = kf.astype(BF16)
        vb = v_ref[lo:hi, :]
        icol, fcol = icol_all[lo:hi, :], fcol_all[lo:hi, :]
        irow, frow = irow_all[:, lo:hi], frow_all[:, lo:hi]

        bcum_col = jnp.sum(jnp.where(tril, frow, 0.0), axis=-1, keepdims=True)
        bcum_row = jnp.sum(jnp.where(r_i <= c_i, fcol, 0.0), axis=0, keepdims=True)
        dmat = jnp.where(tril, bcum_col - bcum_row + irow, -jnp.inf)
        inter = bcum_col + m_prev
        m_t = jnp.maximum(inter, jnp.max(dmat, axis=-1, keepdims=True))
        w_intra = jnp.exp(dmat - m_t)
        w_inter = jnp.exp(inter - m_t)

        qk = lax.dot_general(qb, kb, (((1,), (1,)), ((), ())), preferred_element_type=F32)
        scores = qk * w_intra
        cmat = c_ref[...]
        num = (jnp.dot(scores.astype(BF16), vb, preferred_element_type=F32)
               + w_inter * jnp.dot(qb, cmat.astype(BF16), preferred_element_type=F32))
        den = (jnp.sum(scores, axis=-1, keepdims=True)
               + w_inter * jnp.sum(qf * n_ref[...], axis=-1, keepdims=True))
        hbuf[lo:hi, :] = num / jnp.maximum(jnp.abs(den), jnp.exp(-m_t))

        b_last = bcum_col[CHUNK - 1:CHUNK, :]
        decay = b_last - bcum_col + icol
        m_new = jnp.maximum(b_last + m_prev, jnp.max(decay, axis=0, keepdims=True))
        ws = jnp.exp(decay - m_new)
        carry_scale = jnp.exp(b_last + m_prev - m_new)
        kw = kf * ws
        c_ref[...] = carry_scale * cmat + lax.dot_general(
            kw.astype(BF16), vb, (((0,), (0,)), ((), ())), preferred_element_type=F32)
        n_ref[...] = carry_scale * n_ref[...] + jnp.sum(kw, axis=0, keepdims=True)
        m_prev = m_new

    m_ref[...] = jnp.broadcast_to(m_prev, m_ref.shape)

    hm = hbuf[...]
    hn = hm * lax.rsqrt(jnp.mean(hm * hm, axis=-1, keepdims=True) + EPS) * nw_ref[...]
    og = o_ref[...].astype(F32)
    zg = z_ref[...].astype(F32)
    y_ref[...] = (hn * _sigmoid(og) * (zg * _sigmoid(zg))).astype(BF16)


def _mlstm(proj, gcol, grow, conv_w, conv_b, mnw, batch, seq, ts):
    nt = seq // ts
    hd = M_HEAD_DIM

    def seg_spec(seg):
        return pl.BlockSpec((None, ts, hd), lambda b, h, t: (seg, b * nt + t, h))

    return pl.pallas_call(
        functools.partial(_mlstm_kernel, ts=ts),
        grid=(batch, M_HEADS, nt),
        in_specs=[
            seg_spec(SEG_MQ), seg_spec(SEG_MK), seg_spec(SEG_MV), seg_spec(SEG_MO), seg_spec(SEG_MZ),
            pl.BlockSpec((ts, GATE_PAD), lambda b, h, t: (b * nt + t, 0)),
            pl.BlockSpec((8, ts), lambda b, h, t: (0, b * nt + t)),
            pl.BlockSpec((CONV_WIDTH, hd), lambda b, h, t: (0, h)),
            pl.BlockSpec((CONV_WIDTH, hd), lambda b, h, t: (0, M_HEADS + h)),
            pl.BlockSpec((1, hd), lambda b, h, t: (0, h)),
            pl.BlockSpec((1, hd), lambda b, h, t: (0, M_HEADS + h)),
            pl.BlockSpec((1, hd), lambda b, h, t: (0, h)),
        ],
        out_specs=pl.BlockSpec((ts, hd), lambda b, h, t: (b * nt + t, h)),
        out_shape=jax.ShapeDtypeStruct((batch * seq, M_WIDTH), BF16),
        scratch_shapes=[
            pltpu.VMEM((ts + CONV_HALO, hd), F32),
            pltpu.VMEM((ts + CONV_HALO, hd), F32),
            pltpu.VMEM((ts, hd), F32),
            pltpu.VMEM((hd, hd), F32),
            pltpu.VMEM((1, hd), F32),
            pltpu.VMEM((1, 128), F32),
        ],
        compiler_params=pltpu.CompilerParams(
            dimension_semantics=("arbitrary", "arbitrary", "arbitrary"), vmem_limit_bytes=VMEM_LIMIT),
        name="mlstm",
    )(proj, proj, proj, proj, proj, gcol, grow, conv_w, conv_w, conv_b, conv_b, mnw)


def _sb_kernel(q_ref, k_ref, v_ref, z_ref, qnw_ref, knw_ref, tri_ref,
               y_ref, kn_ref, *, tq):
    qi = pl.program_id(2)
    hd = SB_HEAD_DIM

    @pl.when(qi == 0)
    def _():
        kf = k_ref[...].astype(F32)
        kn = kf * lax.rsqrt(jnp.mean(kf * kf, axis=-1, keepdims=True) + EPS) * knw_ref[...]
        kn_ref[...] = kn.astype(BF16)

    qf = q_ref[...].astype(F32)
    qn = qf * lax.rsqrt(jnp.mean(qf * qf, axis=-1, keepdims=True) + EPS) * qnw_ref[...]
    qb = (qn * (hd ** -0.5)).astype(BF16)
    tri = tri_ref[...]

    def tile(kj, carry, acc, masked):
        start = pl.multiple_of(kj * tq, tq)
        kb = kn_ref[pl.ds(start, tq), :]
        vb = v_ref[pl.ds(start, tq), :]
        z = lax.dot_general(qb, kb, (((1,), (1,)), ((), ())), preferred_element_type=F32)
        l1m = -(jnp.maximum(z, 0.0) + jnp.log(1.0 + jnp.exp(-jnp.abs(z))))
        if masked:
            r_i = lax.broadcasted_iota(jnp.int32, (tq, tq), 0)
            c_i = lax.broadcasted_iota(jnp.int32, (tq, tq), 1)
            causal = c_i < r_i
            l1m = jnp.where(causal, l1m, 0.0)
        csum = jnp.dot(l1m.astype(BF16), tri, preferred_element_type=F32)
        a = jnp.exp(z + csum + carry)
        if masked:
            a = jnp.where(causal, a, 0.0)
        acc = acc + jnp.dot(a.astype(BF16), vb, preferred_element_type=F32)
        carry = carry + csum[:, 0:1]
        return carry, acc

    carry0 = jnp.zeros((tq, 1), F32)
    acc0 = jnp.zeros((tq, hd), F32)
    carry, acc = tile(qi, carry0, acc0, True)

    def body(i, state):
        return tile(qi - 1 - i, state[0], state[1], False)

    carry, acc = lax.fori_loop(0, qi, body, (carry, acc))

    zg = z_ref[...].astype(F32)
    y_ref[...] = (acc * (zg * _sigmoid(zg))).astype(BF16)


def _stickbreak(proj, qnw, knw, tri, batch, seq, tq):
    nq = seq // tq
    hd = SB_HEAD_DIM
    return pl.pallas_call(
        functools.partial(_sb_kernel, tq=tq),
        grid=(batch, SB_HEADS, nq),
        in_specs=[
            pl.BlockSpec((None, tq, hd), lambda b, h, q: (SEG_SQ, b * nq + q, h)),
            pl.BlockSpec((None, seq, hd), lambda b, h, q: (SEG_SK, b, h)),
            pl.BlockSpec((None, seq, hd), lambda b, h, q: (SEG_SV, b, h)),
            pl.BlockSpec((None, tq, hd), lambda b, h, q: (SEG_SZ, b * nq + q, h)),
            pl.BlockSpec((1, hd), lambda b, h, q: (0, 0)),
            pl.BlockSpec((1, hd), lambda b, h, q: (0, 0)),
            pl.BlockSpec((tq, tq), lambda b, h, q: (0, 0)),
        ],
        out_specs=pl.BlockSpec((tq, hd), lambda b, h, q: (b * nq + q, h)),
        out_shape=jax.ShapeDtypeStruct((batch * seq, SB_WIDTH), BF16),
        scratch_shapes=[pltpu.VMEM((seq, hd), BF16)],
        compiler_params=pltpu.CompilerParams(
            dimension_semantics=("arbitrary", "arbitrary", "arbitrary"), vmem_limit_bytes=VMEM_LIMIT),
        name="stickbreak",
    )(proj, proj, proj, proj, qnw, knw, tri)


def _merge_kernel(x_ref, ym_ref, ys_ref, gm_ref, gs_ref, wm_ref, ws_ref, wo_ref, out_ref):
    pm = jnp.dot(ym_ref[...], wm_ref[...], preferred_element_type=F32)
    ps = jnp.dot(ys_ref[...], ws_ref[...], preferred_element_type=F32)
    merged = _sigmoid(gm_ref[...].astype(F32)) * pm + _sigmoid(gs_ref[...].astype(F32)) * ps
    out_ref[...] = x_ref[...] + jnp.dot(merged.astype(BF16), wo_ref[...], preferred_element_type=F32)


def _merge(x2, y_m, y_s, proj, wm, ws, wo, tm):
    m_rows = x2.shape[0]
    row = lambda i: (i, 0)
    const = lambda i: (0, 0)
    return pl.pallas_call(
        _merge_kernel,
        grid=(m_rows // tm,),
        in_specs=[
            pl.BlockSpec((tm, D_MODEL), row),
            pl.BlockSpec((tm, M_WIDTH), row),
            pl.BlockSpec((tm, SB_WIDTH), row),
            pl.BlockSpec((None, tm, SEG_W), lambda i: (SEG_GM, i, 0)),
            pl.BlockSpec((None, tm, SEG_W), lambda i: (SEG_GS, i, 0)),
            pl.BlockSpec((M_WIDTH, D_MODEL), const),
            pl.BlockSpec((SB_WIDTH, D_MODEL), const),
            pl.BlockSpec((D_MODEL, D_MODEL), const),
        ],
        out_specs=pl.BlockSpec((tm, D_MODEL), row),
        out_shape=jax.ShapeDtypeStruct((m_rows, D_MODEL), F32),
        compiler_params=pltpu.CompilerParams(
            dimension_semantics=("arbitrary",), vmem_limit_bytes=VMEM_LIMIT),
        name="merge",
    )(x2, y_m, y_s, proj, proj, wm, ws, wo)


def _split_weights(w_in, b_in):
    mw = M_WIDTH
    g0 = 3 * mw
    g1 = g0 + 2 * M_HEADS
    w_main = jnp.concatenate([w_in[:, :g0], w_in[:, g1:]], axis=1)
    b_main = jnp.concatenate([b_in[:g0], b_in[g1:]])[None, :]
    w_gate = w_in[:, g0:g1]
    b_gate = b_in[g0:g1]
    wg = jnp.pad(w_gate, ((0, 0), (0, GATE_PAD - 2 * M_HEADS)))
    bg = jnp.pad(b_gate, (0, GATE_PAD - 2 * M_HEADS))[None, :]
    wgt = jnp.pad(w_gate.T, ((0, GATE_ROWS - 2 * M_HEADS), (0, 0)))
    bgt = b_gate[:, None]
    return w_main.astype(BF16), b_main, wg.astype(BF16), bg, wgt.astype(BF16), bgt


def kernel(x, norm_w, w_in, b_in, conv_w, conv_b, mlstm_norm_w, sb_q_norm_w, sb_k_norm_w,
           w_proj_m, w_proj_s, w_out):
    batch, seq, d_model = x.shape
    assert d_model == D_MODEL and seq % 256 == 0
    m_rows = batch * seq
    x2 = x.reshape(m_rows, d_model)

    tm = min(1024, m_rows)
    ts = min(512, seq)
    tq = 256

    w_main, b_main, wg, bg, wgt, bgt = _split_weights(w_in, b_in)
    proj, gcol, grow = _inproj(x2, norm_w[None, :], w_main, b_main, wg, bg, wgt, bgt, tm)

    y_m = _mlstm(proj, gcol, grow, conv_w, conv_b[None, :], mlstm_norm_w[None, :], batch, seq, ts)

    tri = (jnp.arange(tq)[:, None] >= jnp.arange(tq)[None, :]).astype(BF16)
    y_s = _stickbreak(proj, sb_q_norm_w[None, :], sb_k_norm_w[None, :], tri, batch, seq, tq)

    out = _merge(x2, y_m, y_s, proj, w_proj_m.astype(BF16), w_proj_s.astype(BF16),
                 w_out.astype(BF16), tm)
    return out.reshape(batch, seq, d_model)
```

```python
import functools

import jax
import jax.numpy as jnp
from jax import lax
from jax.experimental import pallas as pl
from jax.experimental.pallas import tpu as pltpu

F32 = jnp.float32
BF16 = jnp.bfloat16

EPS = 1e-6
D_MODEL = 1024
CHUNK = 64
M_HEADS = 4
M_HEAD_DIM = 256
M_WIDTH = M_HEADS * M_HEAD_DIM
SB_HEADS = 8
SB_HEAD_DIM = 128
SB_WIDTH = SB_HEADS * SB_HEAD_DIM
CONV_WIDTH = 4
N_BRANCH = 2

SEG_MQ, SEG_MK, SEG_MV, SEG_MO, SEG_MZ, SEG_SQ, SEG_SK, SEG_SV, SEG_SZ, SEG_GM, SEG_GS = range(11)
N_SEG = 11
SEG_W = 1024
GATE_PAD = 128
GATE_ROWS = 16
CONV_HALO = 8

SB_GROUP = 4
LOG2E = 1.4426950408889634
SB_DEAD_LOG2 = -160.0

VMEM_LIMIT = 56 * 1024 * 1024


def _logsig(x):
    return jnp.minimum(x, 0.0) - jnp.log1p(jnp.exp(-jnp.abs(x)))


def _sigmoid(x):
    return 1.0 / (1.0 + jnp.exp(-x))


def _inproj_kernel(x_ref, nw_ref, w_ref, b_ref, wg_ref, bg_ref, wgt_ref, bgt_ref,
                   proj_ref, gcol_ref, grow_ref, h_ref):
    @pl.when(pl.program_id(1) == 0)
    def _():
        x = x_ref[...]
        y = x * lax.rsqrt(jnp.mean(x * x, axis=-1, keepdims=True) + EPS) * nw_ref[...]
        hb = y.astype(BF16)
        h_ref[...] = hb
        gcol_ref[...] = jnp.dot(hb, wg_ref[...], preferred_element_type=F32) + bg_ref[...]
        gt = lax.dot_general(wgt_ref[...], hb, (((1,), (1,)), ((), ())), preferred_element_type=F32)
        grow_ref[...] = gt[0:8, :] + bgt_ref[...]

    acc = jnp.dot(h_ref[...], w_ref[...], preferred_element_type=F32)
    proj_ref[...] = (acc + b_ref[...]).astype(BF16)


def _inproj(x2, norm_w, w_main, b_main, wg, bg, wgt, bgt, tm):
    m_rows = x2.shape[0]
    grid = (m_rows // tm, N_SEG)
    return pl.pallas_call(
        _inproj_kernel,
        grid=grid,
        in_specs=[
            pl.BlockSpec((tm, D_MODEL), lambda i, j: (i, 0)),
            pl.BlockSpec((1, D_MODEL), lambda i, j: (0, 0)),
            pl.BlockSpec((D_MODEL, SEG_W), lambda i, j: (0, j)),
            pl.BlockSpec((1, SEG_W), lambda i, j: (0, j)),
            pl.BlockSpec((D_MODEL, GATE_PAD), lambda i, j: (0, 0)),
            pl.BlockSpec((1, GATE_PAD), lambda i, j: (0, 0)),
            pl.BlockSpec((GATE_ROWS, D_MODEL), lambda i, j: (0, 0)),
            pl.BlockSpec((8, 1), lambda i, j: (0, 0)),
        ],
        out_specs=[
            pl.BlockSpec((None, tm, SEG_W), lambda i, j: (j, i, 0)),
            pl.BlockSpec((tm, GATE_PAD), lambda i, j: (i, 0)),
            pl.BlockSpec((8, tm), lambda i, j: (0, i)),
        ],
        out_shape=[
            jax.ShapeDtypeStruct((N_SEG, m_rows, SEG_W), BF16),
            jax.ShapeDtypeStruct((m_rows, GATE_PAD), F32),
            jax.ShapeDtypeStruct((8, m_rows), F32),
        ],
        scratch_shapes=[pltpu.VMEM((tm, D_MODEL), BF16)],
        compiler_params=pltpu.CompilerParams(
            dimension_semantics=("arbitrary", "arbitrary"), vmem_limit_bytes=VMEM_LIMIT),
        name="inproj",
    )(x2, norm_w, w_main, b_main, wg, bg, wgt, bgt)


def _mlstm_kernel(q_ref, k_ref, v_ref, o_ref, z_ref, gcol_ref, grow_ref,
                  cwq_ref, cwk_ref, cbq_ref, cbk_ref, nw_ref,
                  y_ref, qext, kext, hbuf, c_ref, n_ref, m_ref, *, ts):
    head = pl.program_id(1)
    t_idx = pl.program_id(2)
    hd = M_HEAD_DIM

    @pl.when(t_idx == 0)
    def _():
        qext[0:CONV_HALO, :] = jnp.zeros((CONV_HALO, hd), F32)
        kext[0:CONV_HALO, :] = jnp.zeros((CONV_HALO, hd), F32)
        c_ref[...] = jnp.zeros_like(c_ref)
        n_ref[...] = jnp.zeros_like(n_ref)
        m_ref[...] = jnp.zeros_like(m_ref)

    @pl.when(t_idx > 0)
    def _():
        qext[0:CONV_HALO, :] = qext[ts:ts + CONV_HALO, :]
        kext[0:CONV_HALO, :] = kext[ts:ts + CONV_HALO, :]

    qext[CONV_HALO:CONV_HALO + ts, :] = q_ref[...].astype(F32)
    kext[CONV_HALO:CONV_HALO + ts, :] = k_ref[...].astype(F32)

    def conv_silu(ext, cw_ref, cb_ref):
        acc = cb_ref[...]
        for j in range(CONV_WIDTH):
            off = CONV_HALO - (CONV_WIDTH - 1) + j
            acc = acc + cw_ref[j:j + 1, :] * ext[off:off + ts, :]
        return acc * _sigmoid(acc)

    qc = conv_silu(qext, cwq_ref, cbq_ref)
    kc = conv_silu(kext, cwk_ref, cbk_ref) * (hd ** -0.5)

    g = gcol_ref[...]
    lane = lax.broadcasted_iota(jnp.int32, g.shape, 1)
    icol_all = jnp.sum(jnp.where(lane == head, g, 0.0), axis=-1, keepdims=True)
    fcol_all = _logsig(jnp.sum(jnp.where(lane == head + M_HEADS, g, 0.0), axis=-1, keepdims=True))
    irow_all = grow_ref[pl.ds(head, 1), :]
    frow_all = _logsig(grow_ref[pl.ds(head + M_HEADS, 1), :])

    r_i = lax.broadcasted_iota(jnp.int32, (CHUNK, CHUNK), 0)
    c_i = lax.broadcasted_iota(jnp.int32, (CHUNK, CHUNK), 1)
    tril = r_i >= c_i

    m_prev = m_ref[:, 0:1]
    for c in range(ts // CHUNK):
        lo, hi = c * CHUNK, (c + 1) * CHUNK
        qf = qc[lo:hi, :]
        kf = kc[lo:hi, :]
        qb = qf.astype(BF16)
        kb = kf.astype(BF16)
        vb = v_ref[lo:hi, :]
        icol, fcol = icol_all[lo:hi, :], fcol_all[lo:hi, :]
        irow, frow = irow_all[:, lo:hi], frow_all[:, lo:hi]

        bcum_col = jnp.sum(jnp.where(tril, frow, 0.0), axis=-1, keepdims=True)
        bcum_row = jnp.sum(jnp.where(r_i <= c_i, fcol, 0.0), axis=0, keepdims=True)
        dmat = jnp.where(tril, bcum_col - bcum_row + irow, -jnp.inf)
        inter = bcum_col + m_prev
        m_t = jnp.maximum(inter, jnp.max(dmat, axis=-1, keepdims=True))
        w_intra = jnp.exp(dmat - m_t)
        w_inter = jnp.exp(inter - m_t)

        qk = lax.dot_general(qb, kb, (((1,), (1,)), ((), ())), preferred_element_type=F32)
        scores = qk * w_intra
        cmat = c_ref[...]
        num = (jnp.dot(scores.astype(BF16), vb, preferred_element_type=F32)
               + w_inter * jnp.dot(qb, cmat.astype(BF16), preferred_element_type=F32))
        den = (jnp.sum(scores, axis=-1, keepdims=True)
               + w_inter * jnp.sum(qf * n_ref[...], axis=-1, keepdims=True))
        hbuf[lo:hi, :] = num / jnp.maximum(jnp.abs(den), jnp.exp(-m_t))

        b_last = bcum_col[CHUNK - 1:CHUNK, :]
        decay = b_last - bcum_col + icol
        m_new = jnp.maximum(b_last + m_prev, jnp.max(decay, axis=0, keepdims=True))
        ws = jnp.exp(decay - m_new)
        carry_scale = jnp.exp(b_last + m_prev - m_new)
        kw = kf * ws
        c_ref[...] = carry_scale * cmat + lax.dot_general(
            kw.astype(BF16), vb, (((0,), (0,)), ((), ())), preferred_element_type=F32)
        n_ref[...] = carry_scale * n_ref[...] + jnp.sum(kw, axis=0, keepdims=True)
        m_prev = m_new

    m_ref[...] = jnp.broadcast_to(m_prev, m_ref.shape)

    hm = hbuf[...]
    hn = hm * lax.rsqrt(jnp.mean(hm * hm, axis=-1, keepdims=True) + EPS) * nw_ref[...]
    og = o_ref[...].astype(F32)
    zg = z_ref[...].astype(F32)
    y_ref[...] = (hn * _sigmoid(og) * (zg * _sigmoid(zg))).astype(BF16)


def _mlstm(proj, gcol, grow, conv_w, conv_b, mnw, batch, seq, ts):
    nt = seq // ts
    hd = M_HEAD_DIM

    def seg_spec(seg):
        return pl.BlockSpec((None, ts, hd), lambda b, h, t: (seg, b * nt + t, h))

    return pl.pallas_call(
        functools.partial(_mlstm_kernel, ts=ts),
        grid=(batch, M_HEADS, nt),
        in_specs=[
            seg_spec(SEG_MQ), seg_spec(SEG_MK), seg_spec(SEG_MV), seg_spec(SEG_MO), seg_spec(SEG_MZ),
            pl.BlockSpec((ts, GATE_PAD), lambda b, h, t: (b * nt + t, 0)),
            pl.BlockSpec((8, ts), lambda b, h, t: (0, b * nt + t)),
            pl.BlockSpec((CONV_WIDTH, hd), lambda b, h, t: (0, h)),
            pl.BlockSpec((CONV_WIDTH, hd), lambda b, h, t: (0, M_HEADS + h)),
            pl.BlockSpec((1, hd), lambda b, h, t: (0, h)),
            pl.BlockSpec((1, hd), lambda b, h, t: (0, M_HEADS + h)),
            pl.BlockSpec((1, hd), lambda b, h, t: (0, h)),
        ],
        out_specs=pl.BlockSpec((ts, hd), lambda b, h, t: (b * nt + t, h)),
        out_shape=jax.ShapeDtypeStruct((batch * seq, M_WIDTH), BF16),
        scratch_shapes=[
            pltpu.VMEM((ts + CONV_HALO, hd), F32),
            pltpu.VMEM((ts + CONV_HALO, hd), F32),
            pltpu.VMEM((ts, hd), F32),
            pltpu.VMEM((hd, hd), F32),
            pltpu.VMEM((1, hd), F32),
            pltpu.VMEM((1, 128), F32),
        ],
        compiler_params=pltpu.CompilerParams(
            dimension_semantics=("arbitrary", "arbitrary", "arbitrary"), vmem_limit_bytes=VMEM_LIMIT),
        name="mlstm",
    )(proj, proj, proj, proj, proj, gcol, grow, conv_w, conv_w, conv_b, conv_b, mnw)


def _neg_abs(x):
    return pltpu.bitcast(pltpu.bitcast(x, jnp.int32) | jnp.int32(-2 ** 31), F32)


def _sb_kernel(q_ref, k_ref, v_ref, z_ref, qnw_ref, knw_ref, ntri_ref,
               y_ref, kn_ref, vt_ref, acc_ref, carry_ref, z2_ref, *, tq, heads):
    qi = pl.program_id(2)
    hd = SB_HEAD_DIM
    tk = tq
    hs = [slice(h * hd, (h + 1) * hd) for h in range(heads)]
    nt_dims = (((1,), (1,)), ((), ()))

    @pl.when(qi == 0)
    def _():
        for h in range(heads):
            kf = k_ref[:, hs[h]].astype(F32)
            kn = kf * lax.rsqrt(jnp.mean(kf * kf, axis=-1, keepdims=True) + EPS) * knw_ref[...]
            kn_ref[:, hs[h]] = kn.astype(BF16)
            for j in range(vt_ref.shape[0]):
                vt_ref[j, hs[h], :] = v_ref[j * tk:(j + 1) * tk, hs[h]].astype(F32).T.astype(BF16)

    qbs = []
    for h in range(heads):
        qf = q_ref[:, hs[h]].astype(F32)
        qn = qf * lax.rsqrt(jnp.mean(qf * qf, axis=-1, keepdims=True) + EPS) * qnw_ref[...]
        qbs.append((qn * (hd ** -0.5 * LOG2E)).astype(BF16))
    ntri = ntri_ref[...]

    hr = range(heads)

    def scores(kj):
        start = pl.multiple_of(kj * tk, tk)
        return [lax.dot_general(kn_ref[pl.ds(start, tk), hs[h]], qbs[h], nt_dims,
                                preferred_element_type=F32) for h in hr]

    def tile(kj, z2s, next_slot, first):
        sp2s = [jnp.maximum(z2, 0.0) + jnp.log(1.0 + jnp.exp2(_neg_abs(z2))) * LOG2E for z2 in z2s]
        if first:
            s_i = lax.broadcasted_iota(jnp.int32, (tk, tq), 0)
            t_i = lax.broadcasted_iota(jnp.int32, (tk, tq), 1)
            causal = s_i < t_i
            sp2s = [jnp.where(causal, sp2, 0.0) for sp2 in sp2s]
        csums = [jnp.dot(ntri, sp2.astype(BF16), preferred_element_type=F32) for sp2 in sp2s]
        z2n = scores(jnp.maximum(kj - 1, 0))
        for h in hr:
            z2_ref[next_slot, h] = z2n[h]
        if first:
            avs = [jnp.where(causal, jnp.exp2(z2s[h] + csums[h]), 0.0).astype(BF16) for h in hr]
            for h in hr:
                acc_ref[hs[h], :] = jnp.dot(vt_ref[kj, hs[h], :], avs[h], preferred_element_type=F32)
                carry_ref[h:h + 1, :] = csums[h][0:1, :]
        else:
            carries = [carry_ref[h:h + 1, :] for h in hr]
            avs = [jnp.exp2(z2s[h] + csums[h] + carries[h]).astype(BF16) for h in hr]
            for h in hr:
                acc_ref[hs[h], :] += jnp.dot(vt_ref[kj, hs[h], :], avs[h], preferred_element_type=F32)
                carry_ref[h:h + 1, :] = carries[h] + csums[h][0:1, :]

    tile(qi, scores(qi), 0, True)

    def cond(state):
        return jnp.logical_and(state[0] < qi, state[1] > 0)

    def body(state):
        i = state[0]
        slot = lax.rem(i, 2)
        tile(qi - 1 - i, [z2_ref[slot, h] for h in hr], 1 - slot, False)
        alive = jnp.max(carry_ref[0:heads, :]) >= SB_DEAD_LOG2
        return i + 1, alive.astype(jnp.int32)

    lax.while_loop(cond, body, (jnp.int32(0), jnp.int32(1)))

    for h in range(heads):
        zg = z_ref[:, hs[h]].astype(F32)
        y_ref[:, hs[h]] = (acc_ref[hs[h], :].T * (zg * _sigmoid(zg))).astype(BF16)


def _stickbreak(proj, qnw, knw, ntri, batch, seq, tq, heads):
    nq = seq // tq
    hw = heads * SB_HEAD_DIM
    return pl.pallas_call(
        functools.partial(_sb_kernel, tq=tq, heads=heads),
        grid=(batch, SB_HEADS // heads, nq),
        in_specs=[
            pl.BlockSpec((None, tq, hw), lambda b, h, q: (SEG_SQ, b * nq + q, h)),
            pl.BlockSpec((None, seq, hw), lambda b, h, q: (SEG_SK, b, h)),
            pl.BlockSpec((None, seq, hw), lambda b, h, q: (SEG_SV, b, h)),
            pl.BlockSpec((None, tq, hw), lambda b, h, q: (SEG_SZ, b * nq + q, h)),
            pl.BlockSpec((1, SB_HEAD_DIM), lambda b, h, q: (0, 0)),
            pl.BlockSpec((1, SB_HEAD_DIM), lambda b, h, q: (0, 0)),
            pl.BlockSpec((tq, tq), lambda b, h, q: (0, 0)),
        ],
        out_specs=pl.BlockSpec((tq, hw), lambda b, h, q: (b * nq + q, h)),
        out_shape=jax.ShapeDtypeStruct((batch * seq, SB_WIDTH), BF16),
        scratch_shapes=[
            pltpu.VMEM((seq, hw), BF16),
            pltpu.VMEM((seq // tq, hw, tq), BF16),
            pltpu.VMEM((hw, tq), F32),
            pltpu.VMEM((8, tq), F32),
            pltpu.VMEM((2, heads, tq, tq), F32),
        ],
        compiler_params=pltpu.CompilerParams(
            dimension_semantics=("arbitrary", "arbitrary", "arbitrary"), vmem_limit_bytes=VMEM_LIMIT),
        name="stickbreak",
    )(proj, proj, proj, proj, qnw, knw, ntri)


def _merge_kernel(x_ref, ym_ref, ys_ref, gm_ref, gs_ref, wm_ref, ws_ref, wo_ref, out_ref):
    pm = jnp.dot(ym_ref[...], wm_ref[...], preferred_element_type=F32)
    ps = jnp.dot(ys_ref[...], ws_ref[...], preferred_element_type=F32)
    merged = _sigmoid(gm_ref[...].astype(F32)) * pm + _sigmoid(gs_ref[...].astype(F32)) * ps
    out_ref[...] = x_ref[...] + jnp.dot(merged.astype(BF16), wo_ref[...], preferred_element_type=F32)


def _merge(x2, y_m, y_s, proj, wm, ws, wo, tm):
    m_rows = x2.shape[0]
    row = lambda i: (i, 0)
    const = lambda i: (0, 0)
    return pl.pallas_call(
        _merge_kernel,
        grid=(m_rows // tm,),
        in_specs=[
            pl.BlockSpec((tm, D_MODEL), row),
            pl.BlockSpec((tm, M_WIDTH), row),
            pl.BlockSpec((tm, SB_WIDTH), row),
            pl.BlockSpec((None, tm, SEG_W), lambda i: (SEG_GM, i, 0)),
            pl.BlockSpec((None, tm, SEG_W), lambda i: (SEG_GS, i, 0)),
            pl.BlockSpec((M_WIDTH, D_MODEL), const),
            pl.BlockSpec((SB_WIDTH, D_MODEL), const),
            pl.BlockSpec((D_MODEL, D_MODEL), const),
        ],
        out_specs=pl.BlockSpec((tm, D_MODEL), row),
        out_shape=jax.ShapeDtypeStruct((m_rows, D_MODEL), F32),
        compiler_params=pltpu.CompilerParams(
            dimension_semantics=("arbitrary",), vmem_limit_bytes=VMEM_LIMIT),
        name="merge",
    )(x2, y_m, y_s, proj, proj, wm, ws, wo)


def _split_weights(w_in, b_in):
    mw = M_WIDTH
    g0 = 3 * mw
    g1 = g0 + 2 * M_HEADS
    w_main = jnp.concatenate([w_in[:, :g0], w_in[:, g1:]], axis=1)
    b_main = jnp.concatenate([b_in[:g0], b_in[g1:]])[None, :]
    w_gate = w_in[:, g0:g1]
    b_gate = b_in[g0:g1]
    wg = jnp.pad(w_gate, ((0, 0), (0, GATE_PAD - 2 * M_HEADS)))
    bg = jnp.pad(b_gate, (0, GATE_PAD - 2 * M_HEADS))[None, :]
    wgt = jnp.pad(w_gate.T, ((0, GATE_ROWS - 2 * M_HEADS), (0, 0)))
    bgt = b_gate[:, None]
    return w_main.astype(BF16), b_main, wg.astype(BF16), bg, wgt.astype(BF16), bgt


def kernel(x, norm_w, w_in, b_in, conv_w, conv_b, mlstm_norm_w, sb_q_norm_w, sb_k_norm_w,
           w_proj_m, w_proj_s, w_out):
    batch, seq, d_model = x.shape
    assert d_model == D_MODEL and seq % 256 == 0
    m_rows = batch * seq
    x2 = x.reshape(m_rows, d_model)

    tm = min(1024, m_rows)
    ts = min(512, seq)
    tq = 256

    w_main, b_main, wg, bg, wgt, bgt = _split_weights(w_in, b_in)
    proj, gcol, grow = _inproj(x2, norm_w[None, :], w_main, b_main, wg, bg, wgt, bgt, tm)

    y_m = _mlstm(proj, gcol, grow, conv_w, conv_b[None, :], mlstm_norm_w[None, :], batch, seq, ts)

    ntri = -(jnp.arange(tq)[:, None] <= jnp.arange(tq)[None, :]).astype(BF16)
    y_s = _stickbreak(proj, sb_q_norm_w[None, :], sb_k_norm_w[None, :], ntri, batch, seq, tq, SB_GROUP)

    out = _merge(x2, y_m, y_s, proj, w_proj_m.astype(BF16), w_proj_s.astype(BF16),
                 w_out.astype(BF16), tm)
    return out.reshape(batch, seq, d_model)
```

```python
import functools

import jax
import jax.numpy as jnp
from jax import lax
from jax.experimental import pallas as pl
from jax.experimental.pallas import tpu as pltpu

F32 = jnp.float32
BF16 = jnp.bfloat16

EPS = 1e-6
D_MODEL = 1024
CHUNK = 64
M_HEADS = 4
M_HEAD_DIM = 256
M_WIDTH = M_HEADS * M_HEAD_DIM
SB_HEADS = 8
SB_HEAD_DIM = 128
SB_WIDTH = SB_HEADS * SB_HEAD_DIM
CONV_WIDTH = 4
N_BRANCH = 2

IN_MQ, IN_MK, IN_MV, IN_MO, IN_MZ, IN_SQ, IN_SK, IN_SV, IN_SZ, IN_GM, IN_GS = range(11)
N_IN = 11
SEG_MQ, SEG_MK, SEG_MV, SEG_MG, SEG_SQ, SEG_SK, SEG_SV, SEG_SZ, SEG_GM, SEG_GS = range(10)
N_SEG = 10
SEG_W = 1024
GATE_PAD = 128
GATE_ROWS = 16
CONV_HALO = 8
SB_GROUP = 4
LOG2E = 1.4426950408889634
SB_DEAD_LOG2 = -160.0

VMEM_LIMIT = 56 * 1024 * 1024


def _logsig(x):
    return jnp.minimum(x, 0.0) - jnp.log1p(jnp.exp(-jnp.abs(x)))


def _sigmoid(x):
    return 0.5 + 0.5 * jnp.tanh(0.5 * x)


def _silu(x, scale):
    hx = x * (0.5 * scale)
    return hx + hx * jnp.tanh(0.5 * x)


def _inproj_kernel(x_ref, nw_ref, w_ref, b_ref, wg_ref, bg_ref, wgt_ref, bgt_ref, cw_ref, cb_ref,
                   proj_ref, gcol_ref, grow_ref, h_ref, ext_ref, og_ref, *, tm, tiles_per_seq):
    i = pl.program_id(0)
    j = pl.program_id(1)

    @pl.when(j == 0)
    def _():
        x = x_ref[...]
        y = x * lax.rsqrt(jnp.mean(x * x, axis=-1, keepdims=True) + EPS) * nw_ref[...]
        hb = y.astype(BF16)
        h_ref[...] = hb
        gcol_ref[...] = jnp.dot(hb, wg_ref[...], preferred_element_type=F32) + bg_ref[...]
        gt = lax.dot_general(wgt_ref[...], hb, (((1,), (1,)), ((), ())), preferred_element_type=F32)
        grow_ref[...] = gt[0:8, :] + bgt_ref[...]

    def project():
        return jnp.dot(h_ref[...], w_ref[...], preferred_element_type=F32) + b_ref[...]

    def conv_plane(slot, scale):
        ext = ext_ref.at[slot]
        first = lax.rem(i, tiles_per_seq) == 0

        @pl.when(first)
        def _():
            ext[0:CONV_HALO, :] = jnp.zeros((CONV_HALO, SEG_W), F32)

        @pl.when(jnp.logical_not(first))
        def _():
            ext[0:CONV_HALO, :] = ext[tm:tm + CONV_HALO, :]

        ext[CONV_HALO:CONV_HALO + tm, :] = project()
        last = CONV_WIDTH - 1
        acc = cw_ref[last:last + 1, :] * ext[CONV_HALO:CONV_HALO + tm, :] + cb_ref[...]
        for tap in range(last):
            off = CONV_HALO - last + tap
            acc = acc + cw_ref[tap:tap + 1, :] * ext[off:off + tm, :]
        proj_ref[...] = _silu(acc, scale).astype(BF16)

    @pl.when(j == IN_MQ)
    def _():
        conv_plane(0, 1.0)

    @pl.when(j == IN_MK)
    def _():
        conv_plane(1, M_HEAD_DIM ** -0.5)

    @pl.when(j == IN_MO)
    def _():
        og_ref[...] = _sigmoid(project())

    @pl.when(j == IN_MZ)
    def _():
        proj_ref[...] = (og_ref[...] * _silu(project(), 1.0)).astype(BF16)

    @pl.when(j > IN_MZ)
    def _():
        proj_ref[...] = project().astype(BF16)

    @pl.when(j == IN_MV)
    def _():
        proj_ref[...] = project().astype(BF16)


def _inproj(x2, norm_w, w_main, b_main, wg, bg, wgt, bgt, conv_w, conv_b, tm, seq):
    m_rows = x2.shape[0]
    grid = (m_rows // tm, N_IN)
    conv_blk = lambda i, j: (0, jnp.minimum(j, 1))
    return pl.pallas_call(
        functools.partial(_inproj_kernel, tm=tm, tiles_per_seq=seq // tm),
        grid=grid,
        in_specs=[
            pl.BlockSpec((tm, D_MODEL), lambda i, j: (i, 0)),
            pl.BlockSpec((1, D_MODEL), lambda i, j: (0, 0)),
            pl.BlockSpec((D_MODEL, SEG_W), lambda i, j: (0, j)),
            pl.BlockSpec((1, SEG_W), lambda i, j: (0, j)),
            pl.BlockSpec((D_MODEL, GATE_PAD), lambda i, j: (0, 0)),
            pl.BlockSpec((1, GATE_PAD), lambda i, j: (0, 0)),
            pl.BlockSpec((GATE_ROWS, D_MODEL), lambda i, j: (0, 0)),
            pl.BlockSpec((8, 1), lambda i, j: (0, 0)),
            pl.BlockSpec((CONV_WIDTH, SEG_W), conv_blk),
            pl.BlockSpec((1, SEG_W), conv_blk),
        ],
        out_specs=[
            pl.BlockSpec((None, tm, SEG_W), lambda i, j: (j - (j > IN_MO).astype(jnp.int32), i, 0)),
            pl.BlockSpec((tm, GATE_PAD), lambda i, j: (i, 0)),
            pl.BlockSpec((8, tm), lambda i, j: (0, i)),
        ],
        out_shape=[
            jax.ShapeDtypeStruct((N_SEG, m_rows, SEG_W), BF16),
            jax.ShapeDtypeStruct((m_rows, GATE_PAD), F32),
            jax.ShapeDtypeStruct((8, m_rows), F32),
        ],
        scratch_shapes=[
            pltpu.VMEM((tm, D_MODEL), BF16),
            pltpu.VMEM((2, tm + CONV_HALO, SEG_W), F32),
            pltpu.VMEM((tm, SEG_W), F32),
        ],
        compiler_params=pltpu.CompilerParams(
            dimension_semantics=("arbitrary", "arbitrary"), vmem_limit_bytes=VMEM_LIMIT),
        name="inproj",
    )(x2, norm_w, w_main, b_main, wg, bg, wgt, bgt, conv_w, conv_b)


def _mlstm_kernel(q_ref, k_ref, v_ref, og_ref, gcol_ref, grow_ref, nw_ref,
                  y_ref, hbuf, c_ref, n_ref, m_ref, *, ts):
    head = pl.program_id(1)
    t_idx = pl.program_id(2)

    @pl.when(t_idx == 0)
    def _():
        c_ref[...] = jnp.zeros_like(c_ref)
        n_ref[...] = jnp.zeros_like(n_ref)
        m_ref[...] = jnp.zeros_like(m_ref)

    g = gcol_ref[...]
    lane = lax.broadcasted_iota(jnp.int32, g.shape, 1)
    icol_all = jnp.sum(jnp.where(lane == head, g, 0.0), axis=-1, keepdims=True)
    fcol_all = _logsig(jnp.sum(jnp.where(lane == head + M_HEADS, g, 0.0), axis=-1, keepdims=True))
    irow_all = grow_ref[pl.ds(head, 1), :]
    frow_all = _logsig(grow_ref[pl.ds(head + M_HEADS, 1), :])

    r_i = lax.broadcasted_iota(jnp.int32, (CHUNK, CHUNK), 0)
    c_i = lax.broadcasted_iota(jnp.int32, (CHUNK, CHUNK), 1)
    tril = r_i >= c_i

    chunks = range(ts // CHUNK)
    rows = [slice(c * CHUNK, (c + 1) * CHUNK) for c in chunks]
    qbs = [q_ref[r, :] for r in rows]
    vbs = [v_ref[r, :] for r in rows]
    qks = [lax.dot_general(qbs[c], k_ref[rows[c], :], (((1,), (1,)), ((), ())), preferred_element_type=F32)
           for c in chunks]

    bcum_cols = [jnp.sum(jnp.where(tril, frow_all[:, rows[c]], 0.0), axis=-1, keepdims=True) for c in chunks]
    bcum_rows = [jnp.sum(jnp.where(r_i <= c_i, fcol_all[rows[c], :], 0.0), axis=0, keepdims=True)
                 for c in chunks]
    dmats = [jnp.where(tril, bcum_cols[c] - bcum_rows[c] + irow_all[:, rows[c]], -jnp.inf) for c in chunks]
    mxs = [jnp.max(dmat, axis=-1, keepdims=True) for dmat in dmats]
    b_lasts = [bcum_col[CHUNK - 1:CHUNK, :] for bcum_col in bcum_cols]
    decays = [b_lasts[c] - bcum_cols[c] + icol_all[rows[c], :] for c in chunks]
    mlocs = [jnp.max(decay, axis=0, keepdims=True) for decay in decays]
    gates = [(bcum_cols[c], dmats[c], mxs[c], b_lasts[c], decays[c], mlocs[c]) for c in chunks]

    kwls = [k_ref[rows[c], :].astype(F32) * jnp.exp(gates[c][4] - gates[c][5]) for c in chunks]
    us = [lax.dot_general(kwls[c].astype(BF16), vbs[c], (((0,), (0,)), ((), ())), preferred_element_type=F32)
          for c in chunks]
    nss = [jnp.sum(kwl, axis=0, keepdims=True) for kwl in kwls]
    sls = [qks[c] * jnp.exp(gates[c][1] - gates[c][2]) for c in chunks]
    rss = [jnp.sum(sl, axis=-1, keepdims=True) for sl in sls]
    svs = [jnp.dot(sls[c].astype(BF16), vbs[c], preferred_element_type=F32) for c in chunks]
    local = [(qbs[c].astype(F32), qbs[c], gates[c][0], gates[c][2], rss[c], svs[c], gates[c][3], gates[c][5],
              us[c], nss[c]) for c in chunks]

    m_prev = m_ref[:, 0:1]
    cmat = c_ref[...]
    nvec = n_ref[...]
    carried = []
    for qf, qb, bcum_col, mx, rs, sv, b_last, mloc, u, ns in local:
        inter = bcum_col + m_prev
        qc_state = jnp.dot(qb, cmat.astype(BF16), preferred_element_type=F32)
        qn = jnp.sum(qf * nvec, axis=-1, keepdims=True)
        carried.append((inter, qc_state, qn))
        m_new = jnp.maximum(b_last + m_prev, mloc)
        keep = jnp.exp(b_last + m_prev - m_new)
        add = jnp.exp(mloc - m_new)
        cmat = keep * cmat + add * u
        nvec = keep * nvec + add * ns
        m_prev = m_new
    c_ref[...] = cmat
    n_ref[...] = nvec
    m_ref[...] = jnp.broadcast_to(m_prev, m_ref.shape)

    m_ts = [jnp.maximum(carried[c][0], mxs[c]) for c in chunks]
    r_intras = [jnp.exp(mxs[c] - m_ts[c]) for c in chunks]
    r_inters = [jnp.exp(carried[c][0] - m_ts[c]) for c in chunks]
    invs = [1.0 / jnp.maximum(jnp.abs(r_intras[c] * rss[c] + r_inters[c] * carried[c][2]), jnp.exp(-m_ts[c]))
            for c in chunks]
    w_svs = [r_intras[c] * invs[c] for c in chunks]
    w_states = [r_inters[c] * invs[c] for c in chunks]
    for c in chunks:
        hbuf[rows[c], :] = w_svs[c] * svs[c] + w_states[c] * carried[c][1]

    hm = hbuf[...]
    hn = hm * lax.rsqrt(jnp.mean(hm * hm, axis=-1, keepdims=True) + EPS) * nw_ref[...]
    y_ref[...] = (hn * og_ref[...].astype(F32)).astype(BF16)


def _mlstm(proj, gcol, grow, mnw, batch, seq, ts):
    nt = seq // ts
    hd = M_HEAD_DIM

    def seg_spec(seg):
        return pl.BlockSpec((None, ts, hd), lambda b, h, t: (seg, b * nt + t, h))

    return pl.pallas_call(
        functools.partial(_mlstm_kernel, ts=ts),
        grid=(batch, M_HEADS, nt),
        in_specs=[
            seg_spec(SEG_MQ), seg_spec(SEG_MK), seg_spec(SEG_MV), seg_spec(SEG_MG),
            pl.BlockSpec((ts, GATE_PAD), lambda b, h, t: (b * nt + t, 0)),
            pl.BlockSpec((8, ts), lambda b, h, t: (0, b * nt + t)),
            pl.BlockSpec((1, hd), lambda b, h, t: (0, h)),
        ],
        out_specs=pl.BlockSpec((ts, hd), lambda b, h, t: (b * nt + t, h)),
        out_shape=jax.ShapeDtypeStruct((batch * seq, M_WIDTH), BF16),
        scratch_shapes=[
            pltpu.VMEM((ts, hd), F32),
            pltpu.VMEM((hd, hd), F32),
            pltpu.VMEM((1, hd), F32),
            pltpu.VMEM((1, 128), F32),
        ],
        compiler_params=pltpu.CompilerParams(
            dimension_semantics=("arbitrary", "arbitrary", "arbitrary"), vmem_limit_bytes=VMEM_LIMIT),
        name="mlstm",
    )(proj, proj, proj, proj, gcol, grow, mnw)


def _neg_abs(x):
    return pltpu.bitcast(pltpu.bitcast(x, jnp.int32) | jnp.int32(-2 ** 31), F32)


def _sb_kernel(q_ref, k_ref, v_ref, z_ref, qnw_ref, knw_ref, ntri_ref,
               y_ref, kn_ref, vt_ref, acc_ref, carry_ref, z2_ref, *, tq, heads):
    qi = pl.program_id(2)
    hd = SB_HEAD_DIM
    tk = tq
    hs = [slice(h * hd, (h + 1) * hd) for h in range(heads)]
    nt_dims = (((1,), (1,)), ((), ()))

    @pl.when(qi == 0)
    def _():
        for h in range(heads):
            kf = k_ref[:, hs[h]].astype(F32)
            kn = kf * lax.rsqrt(jnp.mean(kf * kf, axis=-1, keepdims=True) + EPS) * knw_ref[...]
            kn_ref[:, hs[h]] = kn.astype(BF16)
            for j in range(vt_ref.shape[0]):
                vt_ref[j, hs[h], :] = v_ref[j * tk:(j + 1) * tk, hs[h]].astype(F32).T.astype(BF16)

    qbs = []
    for h in range(heads):
        qf = q_ref[:, hs[h]].astype(F32)
        qn = qf * lax.rsqrt(jnp.mean(qf * qf, axis=-1, keepdims=True) + EPS) * qnw_ref[...]
        qbs.append((qn * (hd ** -0.5 * LOG2E)).astype(BF16))
    ntri = ntri_ref[...]

    hr = range(heads)

    def scores(kj):
        start = pl.multiple_of(kj * tk, tk)
        return [lax.dot_general(kn_ref[pl.ds(start, tk), hs[h]], qbs[h], nt_dims,
                                preferred_element_type=F32) for h in hr]

    def tile(kj, z2s, next_slot, first):
        sp2s = [jnp.maximum(z2, 0.0) + jnp.log(1.0 + jnp.exp2(_neg_abs(z2))) * LOG2E for z2 in z2s]
        if first:
            s_i = lax.broadcasted_iota(jnp.int32, (tk, tq), 0)
            t_i = lax.broadcasted_iota(jnp.int32, (tk, tq), 1)
            causal = s_i < t_i
            sp2s = [jnp.where(causal, sp2, 0.0) for sp2 in sp2s]
        csums = [jnp.dot(ntri, sp2.astype(BF16), preferred_element_type=F32) for sp2 in sp2s]
        z2n = scores(jnp.maximum(kj - 1, 0))
        for h in hr:
            z2_ref[next_slot, h] = z2n[h]
        if first:
            avs = [jnp.where(causal, jnp.exp2(z2s[h] + csums[h]), 0.0).astype(BF16) for h in hr]
            for h in hr:
                acc_ref[hs[h], :] = jnp.dot(vt_ref[kj, hs[h], :], avs[h], preferred_element_type=F32)
                carry_ref[h:h + 1, :] = csums[h][0:1, :]
        else:
            carries = [carry_ref[h:h + 1, :] for h in hr]
            avs = [jnp.exp2(z2s[h] + csums[h] + carries[h]).astype(BF16) for h in hr]
            for h in hr:
                acc_ref[hs[h], :] += jnp.dot(vt_ref[kj, hs[h], :], avs[h], preferred_element_type=F32)
                carry_ref[h:h + 1, :] = carries[h] + csums[h][0:1, :]

    tile(qi, scores(qi), 0, True)

    def cond(state):
        return jnp.logical_and(state[0] < qi, state[1] > 0)

    def body(state):
        i = state[0]
        slot = lax.rem(i, 2)
        tile(qi - 1 - i, [z2_ref[slot, h] for h in hr], 1 - slot, False)
        alive = jnp.max(carry_ref[0:heads, :]) >= SB_DEAD_LOG2
        return i + 1, alive.astype(jnp.int32)

    lax.while_loop(cond, body, (jnp.int32(0), jnp.int32(1)))

    for h in range(heads):
        zg = z_ref[:, hs[h]].astype(F32)
        y_ref[:, hs[h]] = (acc_ref[hs[h], :].T * _silu(zg, 1.0)).astype(BF16)


def _stickbreak(proj, qnw, knw, ntri, batch, seq, tq, heads):
    nq = seq // tq
    hw = heads * SB_HEAD_DIM
    return pl.pallas_call(
        functools.partial(_sb_kernel, tq=tq, heads=heads),
        grid=(batch, SB_HEADS // heads, nq),
        in_specs=[
            pl.BlockSpec((None, tq, hw), lambda b, h, q: (SEG_SQ, b * nq + q, h)),
            pl.BlockSpec((None, seq, hw), lambda b, h, q: (SEG_SK, b, h)),
            pl.BlockSpec((None, seq, hw), lambda b, h, q: (SEG_SV, b, h)),
            pl.BlockSpec((None, tq, hw), lambda b, h, q: (SEG_SZ, b * nq + q, h)),
            pl.BlockSpec((1, SB_HEAD_DIM), lambda b, h, q: (0, 0)),
            pl.BlockSpec((1, SB_HEAD_DIM), lambda b, h, q: (0, 0)),
            pl.BlockSpec((tq, tq), lambda b, h, q: (0, 0)),
        ],
        out_specs=pl.BlockSpec((tq, hw), lambda b, h, q: (b * nq + q, h)),
        out_shape=jax.ShapeDtypeStruct((batch * seq, SB_WIDTH), BF16),
        scratch_shapes=[
            pltpu.VMEM((seq, hw), BF16),
            pltpu.VMEM((seq // tq, hw, tq), BF16),
            pltpu.VMEM((hw, tq), F32),
            pltpu.VMEM((8, tq), F32),
            pltpu.VMEM((2, heads, tq, tq), F32),
        ],
        compiler_params=pltpu.CompilerParams(
            dimension_semantics=("arbitrary", "arbitrary", "arbitrary"), vmem_limit_bytes=VMEM_LIMIT),
        name="stickbreak",
    )(proj, proj, proj, proj, qnw, knw, ntri)


def _merge_kernel(x_ref, ym_ref, ys_ref, gm_ref, gs_ref, wm_ref, ws_ref, wo_ref, out_ref):
    pm = jnp.dot(ym_ref[...], wm_ref[...], preferred_element_type=F32)
    ps = jnp.dot(ys_ref[...], ws_ref[...], preferred_element_type=F32)
    merged = _sigmoid(gm_ref[...].astype(F32)) * pm + _sigmoid(gs_ref[...].astype(F32)) * ps
    out_ref[...] = x_ref[...] + jnp.dot(merged.astype(BF16), wo_ref[...], preferred_element_type=F32)


def _merge(x2, y_m, y_s, proj, wm, ws, wo, tm):
    m_rows = x2.shape[0]
    row = lambda i: (i, 0)
    const = lambda i: (0, 0)
    return pl.pallas_call(
        _merge_kernel,
        grid=(m_rows // tm,),
        in_specs=[
            pl.BlockSpec((tm, D_MODEL), row),
            pl.BlockSpec((tm, M_WIDTH), row),
            pl.BlockSpec((tm, SB_WIDTH), row),
            pl.BlockSpec((None, tm, SEG_W), lambda i: (SEG_GM, i, 0)),
            pl.BlockSpec((None, tm, SEG_W), lambda i: (SEG_GS, i, 0)),
            pl.BlockSpec((M_WIDTH, D_MODEL), const),
            pl.BlockSpec((SB_WIDTH, D_MODEL), const),
            pl.BlockSpec((D_MODEL, D_MODEL), const),
        ],
        out_specs=pl.BlockSpec((tm, D_MODEL), row),
        out_shape=jax.ShapeDtypeStruct((m_rows, D_MODEL), F32),
        compiler_params=pltpu.CompilerParams(
            dimension_semantics=("arbitrary",), vmem_limit_bytes=VMEM_LIMIT),
        name="merge",
    )(x2, y_m, y_s, proj, proj, wm, ws, wo)


def _split_weights(w_in, b_in):
    mw = M_WIDTH
    g0 = 3 * mw
    g1 = g0 + 2 * M_HEADS
    w_main = jnp.concatenate([w_in[:, :g0], w_in[:, g1:]], axis=1)
    b_main = jnp.concatenate([b_in[:g0], b_in[g1:]])[None, :]
    w_gate = w_in[:, g0:g1]
    b_gate = b_in[g0:g1]
    wg = jnp.pad(w_gate, ((0, 0), (0, GATE_PAD - 2 * M_HEADS)))
    bg = jnp.pad(b_gate, (0, GATE_PAD - 2 * M_HEADS))[None, :]
    wgt = jnp.pad(w_gate.T, ((0, GATE_ROWS - 2 * M_HEADS), (0, 0)))
    bgt = b_gate[:, None]
    return w_main.astype(BF16), b_main, wg.astype(BF16), bg, wgt.astype(BF16), bgt


def kernel(x, norm_w, w_in, b_in, conv_w, conv_b, mlstm_norm_w, sb_q_norm_w, sb_k_norm_w,
           w_proj_m, w_proj_s, w_out):
    batch, seq, d_model = x.shape
    assert d_model == D_MODEL and seq % 256 == 0
    m_rows = batch * seq
    x2 = x.reshape(m_rows, d_model)

    tm = min(1024, seq)
    ts = min(512, seq)
    tq = 256
    assert seq % tm == 0 and seq % ts == 0

    w_main, b_main, wg, bg, wgt, bgt = _split_weights(w_in, b_in)
    proj, gcol, grow = _inproj(x2, norm_w[None, :], w_main, b_main, wg, bg, wgt, bgt,
                               conv_w, conv_b[None, :], tm, seq)

    y_m = _mlstm(proj, gcol, grow, mlstm_norm_w[None, :], batch, seq, ts)

    ntri = -(jnp.arange(tq)[:, None] <= jnp.arange(tq)[None, :]).astype(BF16)
    y_s = _stickbreak(proj, sb_q_norm_w[None, :], sb_k_norm_w[None, :], ntri, batch, seq, tq, SB_GROUP)

    out = _merge(x2, y_m, y_s, proj, w_proj_m.astype(BF16), w_proj_s.astype(BF16),
                 w_out.astype(BF16), tm)
    return out.reshape(batch, seq, d_model)
```

```python
import functools

import jax
import jax.numpy as jnp
from jax import lax
from jax.experimental import pallas as pl
from jax.experimental.pallas import tpu as pltpu

F32 = jnp.float32
BF16 = jnp.bfloat16

EPS = 1e-6
D_MODEL = 1024
CHUNK = 64
M_HEADS = 4
M_HEAD_DIM = 256
M_WIDTH = M_HEADS * M_HEAD_DIM
SB_HEADS = 8
SB_HEAD_DIM = 128
SB_WIDTH = SB_HEADS * SB_HEAD_DIM
CONV_WIDTH = 4
N_BRANCH = 2

IN_MQ, IN_MK, IN_MV, IN_MO, IN_MZ, IN_SQ, IN_SK, IN_SV, IN_SZ, IN_GM, IN_GS = range(11)
N_IN = 11
SEG_MQ, SEG_MK, SEG_MV, SEG_MG, SEG_SQ, SEG_SK, SEG_SV, SEG_SZ, SEG_GM, SEG_GS = range(10)
N_SEG = 10
SEG_W = 1024
GATE_PAD = 128
GATE_ROWS = 16
CONV_HALO = 8
M_GROUP = 4
SB_GROUP = 4
LOG2E = 1.4426950408889634
SB_DEAD_LOG2 = -160.0

VMEM_LIMIT = 56 * 1024 * 1024


def _logsig(x):
    return jnp.minimum(x, 0.0) - jnp.log1p(jnp.exp(-jnp.abs(x)))


def _sigmoid(x):
    return 0.5 + 0.5 * jnp.tanh(0.5 * x)


def _silu(x, scale):
    hx = x * (0.5 * scale)
    return hx + hx * jnp.tanh(0.5 * x)


def _inproj_kernel(x_ref, nw_ref, w_ref, b_ref, wg_ref, bg_ref, wgt_ref, bgt_ref, cw_ref, cb_ref,
                   proj_ref, gcol_ref, grow_ref, h_ref, ext_ref, og_ref, *, tm, tiles_per_seq):
    i = pl.program_id(0)
    j = pl.program_id(1)

    @pl.when(j == 0)
    def _():
        x = x_ref[...]
        y = x * lax.rsqrt(jnp.mean(x * x, axis=-1, keepdims=True) + EPS) * nw_ref[...]
        hb = y.astype(BF16)
        h_ref[...] = hb
        gcol_ref[...] = jnp.dot(hb, wg_ref[...], preferred_element_type=F32) + bg_ref[...]
        gt = lax.dot_general(wgt_ref[...], hb, (((1,), (1,)), ((), ())), preferred_element_type=F32)
        grow_ref[...] = gt[0:8, :] + bgt_ref[...]

    def project():
        return jnp.dot(h_ref[...], w_ref[...], preferred_element_type=F32) + b_ref[...]

    def conv_plane(slot, scale):
        ext = ext_ref.at[slot]
        first = lax.rem(i, tiles_per_seq) == 0

        @pl.when(first)
        def _():
            ext[0:CONV_HALO, :] = jnp.zeros((CONV_HALO, SEG_W), F32)

        @pl.when(jnp.logical_not(first))
        def _():
            ext[0:CONV_HALO, :] = ext[tm:tm + CONV_HALO, :]

        ext[CONV_HALO:CONV_HALO + tm, :] = project()
        last = CONV_WIDTH - 1
        acc = cw_ref[last:last + 1, :] * ext[CONV_HALO:CONV_HALO + tm, :] + cb_ref[...]
        for tap in range(last):
            off = CONV_HALO - last + tap
            acc = acc + cw_ref[tap:tap + 1, :] * ext[off:off + tm, :]
        proj_ref[...] = _silu(acc, scale).astype(BF16)

    @pl.when(j == IN_MQ)
    def _():
        conv_plane(0, 1.0)

    @pl.when(j == IN_MK)
    def _():
        conv_plane(1, M_HEAD_DIM ** -0.5)

    @pl.when(j == IN_MO)
    def _():
        og_ref[...] = _sigmoid(project())

    @pl.when(j == IN_MZ)
    def _():
        proj_ref[...] = (og_ref[...] * _silu(project(), 1.0)).astype(BF16)

    @pl.when(j > IN_MZ)
    def _():
        proj_ref[...] = project().astype(BF16)

    @pl.when(j == IN_MV)
    def _():
        proj_ref[...] = project().astype(BF16)


def _inproj(x2, norm_w, w_main, b_main, wg, bg, wgt, bgt, conv_w, conv_b, tm, seq):
    m_rows = x2.shape[0]
    grid = (m_rows // tm, N_IN)
    conv_blk = lambda i, j: (0, jnp.minimum(j, 1))
    return pl.pallas_call(
        functools.partial(_inproj_kernel, tm=tm, tiles_per_seq=seq // tm),
        grid=grid,
        in_specs=[
            pl.BlockSpec((tm, D_MODEL), lambda i, j: (i, 0)),
            pl.BlockSpec((1, D_MODEL), lambda i, j: (0, 0)),
            pl.BlockSpec((D_MODEL, SEG_W), lambda i, j: (0, j)),
            pl.BlockSpec((1, SEG_W), lambda i, j: (0, j)),
            pl.BlockSpec((D_MODEL, GATE_PAD), lambda i, j: (0, 0)),
            pl.BlockSpec((1, GATE_PAD), lambda i, j: (0, 0)),
            pl.BlockSpec((GATE_ROWS, D_MODEL), lambda i, j: (0, 0)),
            pl.BlockSpec((8, 1), lambda i, j: (0, 0)),
            pl.BlockSpec((CONV_WIDTH, SEG_W), conv_blk),
            pl.BlockSpec((1, SEG_W), conv_blk),
        ],
        out_specs=[
            pl.BlockSpec((None, tm, SEG_W), lambda i, j: (j - (j > IN_MO).astype(jnp.int32), i, 0)),
            pl.BlockSpec((tm, GATE_PAD), lambda i, j: (i, 0)),
            pl.BlockSpec((8, tm), lambda i, j: (0, i)),
        ],
        out_shape=[
            jax.ShapeDtypeStruct((N_SEG, m_rows, SEG_W), BF16),
            jax.ShapeDtypeStruct((m_rows, GATE_PAD), F32),
            jax.ShapeDtypeStruct((8, m_rows), F32),
        ],
        scratch_shapes=[
            pltpu.VMEM((tm, D_MODEL), BF16),
            pltpu.VMEM((2, tm + CONV_HALO, SEG_W), F32),
            pltpu.VMEM((tm, SEG_W), F32),
        ],
        compiler_params=pltpu.CompilerParams(
            dimension_semantics=("arbitrary", "arbitrary"), vmem_limit_bytes=VMEM_LIMIT),
        name="inproj",
    )(x2, norm_w, w_main, b_main, wg, bg, wgt, bgt, conv_w, conv_b)


def _mlstm_kernel(q_ref, k_ref, v_ref, og_ref, gcol_ref, grow_ref, nw_ref,
                  y_ref, hbuf, c_ref, n_ref, m_ref, *, ts, heads):
    head0 = pl.program_id(1) * heads
    t_idx = pl.program_id(2)
    hd = M_HEAD_DIM
    nchunk = ts // CHUNK
    hr = range(heads)
    cols = [slice(h * hd, (h + 1) * hd) for h in hr]
    rows = [slice(c * CHUNK, (c + 1) * CHUNK) for c in range(nchunk)]
    units = [(h, c) for c in range(nchunk) for h in hr]

    @pl.when(t_idx == 0)
    def _():
        c_ref[...] = jnp.zeros_like(c_ref)
        n_ref[...] = jnp.zeros_like(n_ref)
        m_ref[...] = jnp.zeros_like(m_ref)

    g = gcol_ref[...]
    lane = lax.broadcasted_iota(jnp.int32, g.shape, 1)
    icol_all = [jnp.sum(jnp.where(lane == head0 + h, g, 0.0), axis=-1, keepdims=True) for h in hr]
    fcol_all = [_logsig(jnp.sum(jnp.where(lane == head0 + h + M_HEADS, g, 0.0), axis=-1, keepdims=True))
                for h in hr]
    irow_all = [grow_ref[pl.ds(head0 + h, 1), :] for h in hr]
    frow_all = [_logsig(grow_ref[pl.ds(head0 + h + M_HEADS, 1), :]) for h in hr]

    r_i = lax.broadcasted_iota(jnp.int32, (CHUNK, CHUNK), 0)
    c_i = lax.broadcasted_iota(jnp.int32, (CHUNK, CHUNK), 1)
    tril = r_i >= c_i

    qb = {u: q_ref[rows[u[1]], cols[u[0]]] for u in units}
    vb = {u: v_ref[rows[u[1]], cols[u[0]]] for u in units}
    qk = {u: lax.dot_general(qb[u], k_ref[rows[u[1]], cols[u[0]]], (((1,), (1,)), ((), ())),
                             preferred_element_type=F32) for u in units}
    bcum = {u: jnp.sum(jnp.where(tril, frow_all[u[0]][:, rows[u[1]]], 0.0), axis=-1, keepdims=True)
            for u in units}
    bcum_row = {u: jnp.sum(jnp.where(r_i <= c_i, fcol_all[u[0]][rows[u[1]], :], 0.0), axis=0, keepdims=True)
                for u in units}
    dmat = {u: jnp.where(tril, bcum[u] - bcum_row[u] + irow_all[u[0]][:, rows[u[1]]], -jnp.inf) for u in units}
    mx = {u: jnp.max(dmat[u], axis=-1, keepdims=True) for u in units}
    b_last = {u: bcum[u][CHUNK - 1:CHUNK, :] for u in units}
    decay = {u: b_last[u] - bcum[u] + icol_all[u[0]][rows[u[1]], :] for u in units}
    mloc = {u: jnp.max(decay[u], axis=0, keepdims=True) for u in units}
    kwl = {u: k_ref[rows[u[1]], cols[u[0]]].astype(F32) * jnp.exp(decay[u] - mloc[u]) for u in units}
    upd = {u: lax.dot_general(kwl[u].astype(BF16), vb[u], (((0,), (0,)), ((), ())),
                              preferred_element_type=F32) for u in units}
    ksum = {u: jnp.sum(kwl[u], axis=0, keepdims=True) for u in units}
    sl = {u: qk[u] * jnp.exp(dmat[u] - mx[u]) for u in units}
    rs = {u: jnp.sum(sl[u], axis=-1, keepdims=True) for u in units}
    sv = {u: jnp.dot(sl[u].astype(BF16), vb[u], preferred_element_type=F32) for u in units}

    m_prev = [m_ref[h][:, 0:1] for h in hr]
    cmat = [c_ref[h] for h in hr]
    nvec = [n_ref[h] for h in hr]
    inter, q_state, q_n = {}, {}, {}
    for u in units:
        h = u[0]
        inter[u] = bcum[u] + m_prev[h]
        q_state[u] = jnp.dot(qb[u], cmat[h].astype(BF16), preferred_element_type=F32)
        q_n[u] = jnp.sum(qb[u].astype(F32) * nvec[h], axis=-1, keepdims=True)
        m_new = jnp.maximum(b_last[u] + m_prev[h], mloc[u])
        keep = jnp.exp(b_last[u] + m_prev[h] - m_new)
        add = jnp.exp(mloc[u] - m_new)
        cmat[h] = keep * cmat[h] + add * upd[u]
        nvec[h] = keep * nvec[h] + add * ksum[u]
        m_prev[h] = m_new
    for h in hr:
        c_ref[h] = cmat[h]
        n_ref[h] = nvec[h]
        m_ref[h] = jnp.broadcast_to(m_prev[h], m_ref.shape[1:])

    m_t = {u: jnp.maximum(inter[u], mx[u]) for u in units}
    r_intra = {u: jnp.exp(mx[u] - m_t[u]) for u in units}
    r_inter = {u: jnp.exp(inter[u] - m_t[u]) for u in units}
    inv = {u: 1.0 / jnp.maximum(jnp.abs(r_intra[u] * rs[u] + r_inter[u] * q_n[u]), jnp.exp(-m_t[u]))
           for u in units}
    w_sv = {u: r_intra[u] * inv[u] for u in units}
    w_state = {u: r_inter[u] * inv[u] for u in units}
    for u in units:
        hbuf[rows[u[1]], cols[u[0]]] = w_sv[u] * sv[u] + w_state[u] * q_state[u]

    for h in hr:
        hm = hbuf[:, cols[h]]
        hn = hm * lax.rsqrt(jnp.mean(hm * hm, axis=-1, keepdims=True) + EPS) * nw_ref[:, cols[h]]
        y_ref[:, cols[h]] = (hn * og_ref[:, cols[h]].astype(F32)).astype(BF16)


def _mlstm(proj, gcol, grow, mnw, batch, seq, ts, heads):
    nt = seq // ts
    hw = heads * M_HEAD_DIM

    def seg_spec(seg):
        return pl.BlockSpec((None, ts, hw), lambda b, h, t: (seg, b * nt + t, h))

    return pl.pallas_call(
        functools.partial(_mlstm_kernel, ts=ts, heads=heads),
        grid=(batch, M_HEADS // heads, nt),
        in_specs=[
            seg_spec(SEG_MQ), seg_spec(SEG_MK), seg_spec(SEG_MV), seg_spec(SEG_MG),
            pl.BlockSpec((ts, GATE_PAD), lambda b, h, t: (b * nt + t, 0)),
            pl.BlockSpec((8, ts), lambda b, h, t: (0, b * nt + t)),
            pl.BlockSpec((1, hw), lambda b, h, t: (0, h)),
        ],
        out_specs=pl.BlockSpec((ts, hw), lambda b, h, t: (b * nt + t, h)),
        out_shape=jax.ShapeDtypeStruct((batch * seq, M_WIDTH), BF16),
        scratch_shapes=[
            pltpu.VMEM((ts, hw), F32),
            pltpu.VMEM((heads, M_HEAD_DIM, M_HEAD_DIM), F32),
            pltpu.VMEM((heads, 1, M_HEAD_DIM), F32),
            pltpu.VMEM((heads, 1, 128), F32),
        ],
        compiler_params=pltpu.CompilerParams(
            dimension_semantics=("arbitrary", "arbitrary", "arbitrary"), vmem_limit_bytes=VMEM_LIMIT),
        name="mlstm",
    )(proj, proj, proj, proj, gcol, grow, mnw)


def _neg_abs(x):
    return pltpu.bitcast(pltpu.bitcast(x, jnp.int32) | jnp.int32(-2 ** 31), F32)


def _sb_kernel(q_ref, k_ref, v_ref, z_ref, qnw_ref, knw_ref, ntri_ref,
               y_ref, kn_ref, vt_ref, acc_ref, carry_ref, z2_ref, *, tq, heads):
    qi = pl.program_id(2)
    hd = SB_HEAD_DIM
    tk = tq
    hs = [slice(h * hd, (h + 1) * hd) for h in range(heads)]
    nt_dims = (((1,), (1,)), ((), ()))

    @pl.when(qi == 0)
    def _():
        for h in range(heads):
            kf = k_ref[:, hs[h]].astype(F32)
            kn = kf * lax.rsqrt(jnp.mean(kf * kf, axis=-1, keepdims=True) + EPS) * knw_ref[...]
            kn_ref[:, hs[h]] = kn.astype(BF16)
            for j in range(vt_ref.shape[0]):
                vt_ref[j, hs[h], :] = v_ref[j * tk:(j + 1) * tk, hs[h]].astype(F32).T.astype(BF16)

    qbs = []
    for h in range(heads):
        qf = q_ref[:, hs[h]].astype(F32)
        qn = qf * lax.rsqrt(jnp.mean(qf * qf, axis=-1, keepdims=True) + EPS) * qnw_ref[...]
        qbs.append((qn * (hd ** -0.5 * LOG2E)).astype(BF16))
    ntri = ntri_ref[...]

    hr = range(heads)

    def scores(kj):
        start = pl.multiple_of(kj * tk, tk)
        return [lax.dot_general(kn_ref[pl.ds(start, tk), hs[h]], qbs[h], nt_dims,
                                preferred_element_type=F32) for h in hr]

    def tile(kj, z2s, next_slot, first):
        sp2s = [jnp.maximum(z2, 0.0) + jnp.log(1.0 + jnp.exp2(_neg_abs(z2))) * LOG2E for z2 in z2s]
        if first:
            s_i = lax.broadcasted_iota(jnp.int32, (tk, tq), 0)
            t_i = lax.broadcasted_iota(jnp.int32, (tk, tq), 1)
            causal = s_i < t_i
            sp2s = [jnp.where(causal, sp2, 0.0) for sp2 in sp2s]
        csums = [jnp.dot(ntri, sp2.astype(BF16), preferred_element_type=F32) for sp2 in sp2s]
        z2n = scores(jnp.maximum(kj - 1, 0))
        for h in hr:
            z2_ref[next_slot, h] = z2n[h]
        if first:
            avs = [jnp.where(causal, jnp.exp2(z2s[h] + csums[h]), 0.0).astype(BF16) for h in hr]
            for h in hr:
                acc_ref[hs[h], :] = jnp.dot(vt_ref[kj, hs[h], :], avs[h], preferred_element_type=F32)
                carry_ref[h:h + 1, :] = csums[h][0:1, :]
        else:
            carries = [carry_ref[h:h + 1, :] for h in hr]
            avs = [jnp.exp2(z2s[h] + csums[h] + carries[h]).astype(BF16) for h in hr]
            for h in hr:
                acc_ref[hs[h], :] += jnp.dot(vt_ref[kj, hs[h], :], avs[h], preferred_element_type=F32)
                carry_ref[h:h + 1, :] = carries[h] + csums[h][0:1, :]

    tile(qi, scores(qi), 0, True)

    def cond(state):
        return jnp.logical_and(state[0] < qi, state[1] > 0)

    def body(state):
        i = state[0]
        slot = lax.rem(i, 2)
        tile(qi - 1 - i, [z2_ref[slot, h] for h in hr], 1 - slot, False)
        alive = jnp.max(carry_ref[0:heads, :]) >= SB_DEAD_LOG2
        return i + 1, alive.astype(jnp.int32)

    lax.while_loop(cond, body, (jnp.int32(0), jnp.int32(1)))

    for h in range(heads):
        zg = z_ref[:, hs[h]].astype(F32)
        y_ref[:, hs[h]] = (acc_ref[hs[h], :].T * _silu(zg, 1.0)).astype(BF16)


def _stickbreak(proj, qnw, knw, ntri, batch, seq, tq, heads):
    nq = seq // tq
    hw = heads * SB_HEAD_DIM
    return pl.pallas_call(
        functools.partial(_sb_kernel, tq=tq, heads=heads),
        grid=(batch, SB_HEADS // heads, nq),
        in_specs=[
            pl.BlockSpec((None, tq, hw), lambda b, h, q: (SEG_SQ, b * nq + q, h)),
            pl.BlockSpec((None, seq, hw), lambda b, h, q: (SEG_SK, b, h)),
            pl.BlockSpec((None, seq, hw), lambda b, h, q: (SEG_SV, b, h)),
            pl.BlockSpec((None, tq, hw), lambda b, h, q: (SEG_SZ, b * nq + q, h)),
            pl.BlockSpec((1, SB_HEAD_DIM), lambda b, h, q: (0, 0)),
            pl.BlockSpec((1, SB_HEAD_DIM), lambda b, h, q: (0, 0)),
            pl.BlockSpec((tq, tq), lambda b, h, q: (0, 0)),
        ],
        out_specs=pl.BlockSpec((tq, hw), lambda b, h, q: (b * nq + q, h)),
        out_shape=jax.ShapeDtypeStruct((batch * seq, SB_WIDTH), BF16),
        scratch_shapes=[
            pltpu.VMEM((seq, hw), BF16),
            pltpu.VMEM((seq // tq, hw, tq), BF16),
            pltpu.VMEM((hw, tq), F32),
            pltpu.VMEM((8, tq), F32),
            pltpu.VMEM((2, heads, tq, tq), F32),
        ],
        compiler_params=pltpu.CompilerParams(
            dimension_semantics=("arbitrary", "arbitrary", "arbitrary"), vmem_limit_bytes=VMEM_LIMIT),
        name="stickbreak",
    )(proj, proj, proj, proj, qnw, knw, ntri)


def _merge_kernel(x_ref, ym_ref, ys_ref, gm_ref, gs_ref, wm_ref, ws_ref, wo_ref, out_ref):
    pm = jnp.dot(ym_ref[...], wm_ref[...], preferred_element_type=F32)
    ps = jnp.dot(ys_ref[...], ws_ref[...], preferred_element_type=F32)
    merged = _sigmoid(gm_ref[...].astype(F32)) * pm + _sigmoid(gs_ref[...].astype(F32)) * ps
    out_ref[...] = x_ref[...] + jnp.dot(merged.astype(BF16), wo_ref[...], preferred_element_type=F32)


def _merge(x2, y_m, y_s, proj, wm, ws, wo, tm):
    m_rows = x2.shape[0]
    row = lambda i: (i, 0)
    const = lambda i: (0, 0)
    return pl.pallas_call(
        _merge_kernel,
        grid=(m_rows // tm,),
        in_specs=[
            pl.BlockSpec((tm, D_MODEL), row),
            pl.BlockSpec((tm, M_WIDTH), row),
            pl.BlockSpec((tm, SB_WIDTH), row),
            pl.BlockSpec((None, tm, SEG_W), lambda i: (SEG_GM, i, 0)),
            pl.BlockSpec((None, tm, SEG_W), lambda i: (SEG_GS, i, 0)),
            pl.BlockSpec((M_WIDTH, D_MODEL), const),
            pl.BlockSpec((SB_WIDTH, D_MODEL), const),
            pl.BlockSpec((D_MODEL, D_MODEL), const),
        ],
        out_specs=pl.BlockSpec((tm, D_MODEL), row),
        out_shape=jax.ShapeDtypeStruct((m_rows, D_MODEL), F32),
        compiler_params=pltpu.CompilerParams(
            dimension_semantics=("arbitrary",), vmem_limit_bytes=VMEM_LIMIT),
        name="merge",
    )(x2, y_m, y_s, proj, proj, wm, ws, wo)


def _split_weights(w_in, b_in):
    mw = M_WIDTH
    g0 = 3 * mw
    g1 = g0 + 2 * M_HEADS
    w_main = jnp.concatenate([w_in[:, :g0], w_in[:, g1:]], axis=1)
    b_main = jnp.concatenate([b_in[:g0], b_in[g1:]])[None, :]
    w_gate = w_in[:, g0:g1]
    b_gate = b_in[g0:g1]
    wg = jnp.pad(w_gate, ((0, 0), (0, GATE_PAD - 2 * M_HEADS)))
    bg = jnp.pad(b_gate, (0, GATE_PAD - 2 * M_HEADS))[None, :]
    wgt = jnp.pad(w_gate.T, ((0, GATE_ROWS - 2 * M_HEADS), (0, 0)))
    bgt = b_gate[:, None]
    return w_main.astype(BF16), b_main, wg.astype(BF16), bg, wgt.astype(BF16), bgt


def kernel(x, norm_w, w_in, b_in, conv_w, conv_b, mlstm_norm_w, sb_q_norm_w, sb_k_norm_w,
           w_proj_m, w_proj_s, w_out):
    batch, seq, d_model = x.shape
    assert d_model == D_MODEL and seq % 256 == 0
    m_rows = batch * seq
    x2 = x.reshape(m_rows, d_model)

    tm = min(1024, seq)
    ts = min(512, seq)
    tq = 256
    assert seq % tm == 0 and seq % ts == 0

    w_main, b_main, wg, bg, wgt, bgt = _split_weights(w_in, b_in)
    proj, gcol, grow = _inproj(x2, norm_w[None, :], w_main, b_main, wg, bg, wgt, bgt,
                               conv_w, conv_b[None, :], tm, seq)

    y_m = _mlstm(proj, gcol, grow, mlstm_norm_w[None, :], batch, seq, ts, M_GROUP)

    ntri = -(jnp.arange(tq)[:, None] <= jnp.arange(tq)[None, :]).astype(BF16)
    y_s = _stickbreak(proj, sb_q_norm_w[None, :], sb_k_norm_w[None, :], ntri, batch, seq, tq, SB_GROUP)

    out = _merge(x2, y_m, y_s, proj, w_proj_m.astype(BF16), w_proj_s.astype(BF16),
                 w_out.astype(BF16), tm)
    return out.reshape(batch, seq, d_model)
```

```python
import functools

import jax
import jax.numpy as jnp
from jax import lax
from jax.experimental import pallas as pl
from jax.experimental.pallas import tpu as pltpu

F32 = jnp.float32
BF16 = jnp.bfloat16

EPS = 1e-6
D_MODEL = 1024
CHUNK = 64
M_HEADS = 4
M_HEAD_DIM = 256
M_WIDTH = M_HEADS * M_HEAD_DIM
SB_HEADS = 8
SB_HEAD_DIM = 128
SB_WIDTH = SB_HEADS * SB_HEAD_DIM
CONV_WIDTH = 4
N_BRANCH = 2

IN_MQ, IN_MK, IN_MV, IN_MO, IN_MZ, IN_SQ, IN_SK, IN_SV, IN_SZ, IN_GM, IN_GS = range(11)
N_IN = 11
SEG_MQ, SEG_MK, SEG_MV, SEG_MG, SEG_SQ, SEG_SK, SEG_SV, SEG_SZ, SEG_GM, SEG_GS = range(10)
N_SEG = 10
SEG_W = 1024
GATE_PAD = 128
GATE_ROWS = 16
CONV_HALO = 8
M_GROUP = 4
SB_GROUP = 4
LOG2E = 1.4426950408889634
SB_DEAD_LOG2 = -160.0

VMEM_LIMIT = 56 * 1024 * 1024


def _logsig(x):
    return jnp.minimum(x, 0.0) - jnp.log1p(jnp.exp(-jnp.abs(x)))


def _sigmoid(x):
    return 0.5 + 0.5 * jnp.tanh(0.5 * x)


def _silu(x, scale):
    hx = x * (0.5 * scale)
    return hx + hx * jnp.tanh(0.5 * x)


def _inproj_kernel(x_ref, nw_ref, w_ref, b_ref, wg_ref, bg_ref, wgt_ref, bgt_ref,
                   proj_ref, gcol_ref, grow_ref, h_ref, og_ref):
    j = pl.program_id(1)

    @pl.when(j == 0)
    def _():
        x = x_ref[...]
        y = x * lax.rsqrt(jnp.mean(x * x, axis=-1, keepdims=True) + EPS) * nw_ref[...]
        hb = y.astype(BF16)
        h_ref[...] = hb
        gcol_ref[...] = jnp.dot(hb, wg_ref[...], preferred_element_type=F32) + bg_ref[...]
        gt = lax.dot_general(wgt_ref[...], hb, (((1,), (1,)), ((), ())), preferred_element_type=F32)
        grow_ref[...] = gt[0:8, :] + bgt_ref[...]

    def project():
        return jnp.dot(h_ref[...], w_ref[...], preferred_element_type=F32) + b_ref[...]

    @pl.when(j == IN_MO)
    def _():
        og_ref[...] = _sigmoid(project())

    @pl.when(j == IN_MZ)
    def _():
        proj_ref[...] = (og_ref[...] * _silu(project(), 1.0)).astype(BF16)

    @pl.when(jnp.logical_and(j != IN_MO, j != IN_MZ))
    def _():
        proj_ref[...] = project().astype(BF16)


def _inproj(x2, norm_w, w_main, b_main, wg, bg, wgt, bgt, tm):
    m_rows = x2.shape[0]
    grid = (m_rows // tm, N_IN)
    return pl.pallas_call(
        _inproj_kernel,
        grid=grid,
        in_specs=[
            pl.BlockSpec((tm, D_MODEL), lambda i, j: (i, 0)),
            pl.BlockSpec((1, D_MODEL), lambda i, j: (0, 0)),
            pl.BlockSpec((D_MODEL, SEG_W), lambda i, j: (0, j)),
            pl.BlockSpec((1, SEG_W), lambda i, j: (0, j)),
            pl.BlockSpec((D_MODEL, GATE_PAD), lambda i, j: (0, 0)),
            pl.BlockSpec((1, GATE_PAD), lambda i, j: (0, 0)),
            pl.BlockSpec((GATE_ROWS, D_MODEL), lambda i, j: (0, 0)),
            pl.BlockSpec((8, 1), lambda i, j: (0, 0)),
        ],
        out_specs=[
            pl.BlockSpec((None, tm, SEG_W), lambda i, j: (j - (j > IN_MO).astype(jnp.int32), i, 0)),
            pl.BlockSpec((tm, GATE_PAD), lambda i, j: (i, 0)),
            pl.BlockSpec((8, tm), lambda i, j: (0, i)),
        ],
        out_shape=[
            jax.ShapeDtypeStruct((N_SEG, m_rows, SEG_W), BF16),
            jax.ShapeDtypeStruct((m_rows, GATE_PAD), F32),
            jax.ShapeDtypeStruct((8, m_rows), F32),
        ],
        scratch_shapes=[
            pltpu.VMEM((tm, D_MODEL), BF16),
            pltpu.VMEM((tm, SEG_W), F32),
        ],
        compiler_params=pltpu.CompilerParams(
            dimension_semantics=("arbitrary", "arbitrary"), vmem_limit_bytes=VMEM_LIMIT),
        name="inproj",
    )(x2, norm_w, w_main, b_main, wg, bg, wgt, bgt)


def _mlstm_kernel(qp_ref, kp_ref, v_ref, og_ref, gcol_ref, grow_ref, cwq_ref, cwk_ref, cbq_ref, cbk_ref, nw_ref,
                  y_ref, qext, kext, q_ref, k_ref, hbuf, c_ref, n_ref, m_ref, *, ts, heads):
    head0 = pl.program_id(1) * heads
    t_idx = pl.program_id(2)
    hd = M_HEAD_DIM
    nchunk = ts // CHUNK
    hr = range(heads)
    cols = [slice(h * hd, (h + 1) * hd) for h in hr]
    rows = [slice(c * CHUNK, (c + 1) * CHUNK) for c in range(nchunk)]
    units = [(h, c) for c in range(nchunk) for h in hr]

    @pl.when(t_idx == 0)
    def _():
        qext[0:CONV_HALO, :] = jnp.zeros((CONV_HALO, qext.shape[1]), F32)
        kext[0:CONV_HALO, :] = jnp.zeros((CONV_HALO, kext.shape[1]), F32)
        c_ref[...] = jnp.zeros_like(c_ref)
        n_ref[...] = jnp.zeros_like(n_ref)
        m_ref[...] = jnp.zeros_like(m_ref)

    @pl.when(t_idx > 0)
    def _():
        qext[0:CONV_HALO, :] = qext[ts:ts + CONV_HALO, :]
        kext[0:CONV_HALO, :] = kext[ts:ts + CONV_HALO, :]

    qext[CONV_HALO:CONV_HALO + ts, :] = qp_ref[...].astype(F32)
    kext[CONV_HALO:CONV_HALO + ts, :] = kp_ref[...].astype(F32)

    def conv_silu(ext, cw_ref, cb_ref, scale):
        last = CONV_WIDTH - 1
        acc = cw_ref[last:last + 1, :] * ext[CONV_HALO:CONV_HALO + ts, :] + cb_ref[...]
        for tap in range(last):
            off = CONV_HALO - last + tap
            acc = acc + cw_ref[tap:tap + 1, :] * ext[off:off + ts, :]
        return _silu(acc, scale).astype(BF16)

    q_ref[...] = conv_silu(qext, cwq_ref, cbq_ref, 1.0)
    k_ref[...] = conv_silu(kext, cwk_ref, cbk_ref, hd ** -0.5)

    g = gcol_ref[...]
    lane = lax.broadcasted_iota(jnp.int32, g.shape, 1)
    icol_all = [jnp.sum(jnp.where(lane == head0 + h, g, 0.0), axis=-1, keepdims=True) for h in hr]
    fcol_all = [_logsig(jnp.sum(jnp.where(lane == head0 + h + M_HEADS, g, 0.0), axis=-1, keepdims=True))
                for h in hr]
    irow_all = [grow_ref[pl.ds(head0 + h, 1), :] for h in hr]
    frow_all = [_logsig(grow_ref[pl.ds(head0 + h + M_HEADS, 1), :]) for h in hr]

    r_i = lax.broadcasted_iota(jnp.int32, (CHUNK, CHUNK), 0)
    c_i = lax.broadcasted_iota(jnp.int32, (CHUNK, CHUNK), 1)
    tril = r_i >= c_i

    qb = {u: q_ref[rows[u[1]], cols[u[0]]] for u in units}
    vb = {u: v_ref[rows[u[1]], cols[u[0]]] for u in units}
    qk = {u: lax.dot_general(qb[u], k_ref[rows[u[1]], cols[u[0]]], (((1,), (1,)), ((), ())),
                             preferred_element_type=F32) for u in units}
    bcum = {u: jnp.sum(jnp.where(tril, frow_all[u[0]][:, rows[u[1]]], 0.0), axis=-1, keepdims=True)
            for u in units}
    bcum_row = {u: jnp.sum(jnp.where(r_i <= c_i, fcol_all[u[0]][rows[u[1]], :], 0.0), axis=0, keepdims=True)
                for u in units}
    dmat = {u: jnp.where(tril, bcum[u] - bcum_row[u] + irow_all[u[0]][:, rows[u[1]]], -jnp.inf) for u in units}
    mx = {u: jnp.max(dmat[u], axis=-1, keepdims=True) for u in units}
    b_last = {u: bcum[u][CHUNK - 1:CHUNK, :] for u in units}
    decay = {u: b_last[u] - bcum[u] + icol_all[u[0]][rows[u[1]], :] for u in units}
    mloc = {u: jnp.max(decay[u], axis=0, keepdims=True) for u in units}
    kwl = {u: k_ref[rows[u[1]], cols[u[0]]].astype(F32) * jnp.exp(decay[u] - mloc[u]) for u in units}
    upd = {u: lax.dot_general(kwl[u].astype(BF16), vb[u], (((0,), (0,)), ((), ())),
                              preferred_element_type=F32) for u in units}
    ksum = {u: jnp.sum(kwl[u], axis=0, keepdims=True) for u in units}
    sl = {u: qk[u] * jnp.exp(dmat[u] - mx[u]) for u in units}
    rs = {u: jnp.sum(sl[u], axis=-1, keepdims=True) for u in units}
    sv = {u: jnp.dot(sl[u].astype(BF16), vb[u], preferred_element_type=F32) for u in units}

    m_prev = [m_ref[h][:, 0:1] for h in hr]
    cmat = [c_ref[h] for h in hr]
    nvec = [n_ref[h] for h in hr]
    inter, q_state, q_n = {}, {}, {}
    for u in units:
        h = u[0]
        inter[u] = bcum[u] + m_prev[h]
        q_state[u] = jnp.dot(qb[u], cmat[h].astype(BF16), preferred_element_type=F32)
        q_n[u] = jnp.sum(qb[u].astype(F32) * nvec[h], axis=-1, keepdims=True)
        m_new = jnp.maximum(b_last[u] + m_prev[h], mloc[u])
        keep = jnp.exp(b_last[u] + m_prev[h] - m_new)
        add = jnp.exp(mloc[u] - m_new)
        cmat[h] = keep * cmat[h] + add * upd[u]
        nvec[h] = keep * nvec[h] + add * ksum[u]
        m_prev[h] = m_new
    for h in hr:
        c_ref[h] = cmat[h]
        n_ref[h] = nvec[h]
        m_ref[h] = jnp.broadcast_to(m_prev[h], m_ref.shape[1:])

    m_t = {u: jnp.maximum(inter[u], mx[u]) for u in units}
    r_intra = {u: jnp.exp(mx[u] - m_t[u]) for u in units}
    r_inter = {u: jnp.exp(inter[u] - m_t[u]) for u in units}
    inv = {u: 1.0 / jnp.maximum(jnp.abs(r_intra[u] * rs[u] + r_inter[u] * q_n[u]), jnp.exp(-m_t[u]))
           for u in units}
    w_sv = {u: r_intra[u] * inv[u] for u in units}
    w_state = {u: r_inter[u] * inv[u] for u in units}
    for u in units:
        hbuf[rows[u[1]], cols[u[0]]] = w_sv[u] * sv[u] + w_state[u] * q_state[u]

    for h in hr:
        hm = hbuf[:, cols[h]]
        hn = hm * lax.rsqrt(jnp.mean(hm * hm, axis=-1, keepdims=True) + EPS) * nw_ref[:, cols[h]]
        y_ref[:, cols[h]] = (hn * og_ref[:, cols[h]].astype(F32)).astype(BF16)


def _mlstm(proj, gcol, grow, conv_w, conv_b, mnw, batch, seq, ts, heads):
    nt = seq // ts
    hw = heads * M_HEAD_DIM

    def seg_spec(seg):
        return pl.BlockSpec((None, ts, hw), lambda b, h, t: (seg, b * nt + t, h))

    return pl.pallas_call(
        functools.partial(_mlstm_kernel, ts=ts, heads=heads),
        grid=(batch, M_HEADS // heads, nt),
        in_specs=[
            seg_spec(SEG_MQ), seg_spec(SEG_MK), seg_spec(SEG_MV), seg_spec(SEG_MG),
            pl.BlockSpec((ts, GATE_PAD), lambda b, h, t: (b * nt + t, 0)),
            pl.BlockSpec((8, ts), lambda b, h, t: (0, b * nt + t)),
            pl.BlockSpec((CONV_WIDTH, hw), lambda b, h, t: (0, h)),
            pl.BlockSpec((CONV_WIDTH, hw), lambda b, h, t: (0, M_HEADS // heads + h)),
            pl.BlockSpec((1, hw), lambda b, h, t: (0, h)),
            pl.BlockSpec((1, hw), lambda b, h, t: (0, M_HEADS // heads + h)),
            pl.BlockSpec((1, hw), lambda b, h, t: (0, h)),
        ],
        out_specs=pl.BlockSpec((ts, hw), lambda b, h, t: (b * nt + t, h)),
        out_shape=jax.ShapeDtypeStruct((batch * seq, M_WIDTH), BF16),
        scratch_shapes=[
            pltpu.VMEM((ts + CONV_HALO, hw), F32),
            pltpu.VMEM((ts + CONV_HALO, hw), F32),
            pltpu.VMEM((ts, hw), BF16),
            pltpu.VMEM((ts, hw), BF16),
            pltpu.VMEM((ts, hw), F32),
            pltpu.VMEM((heads, M_HEAD_DIM, M_HEAD_DIM), F32),
            pltpu.VMEM((heads, 1, M_HEAD_DIM), F32),
            pltpu.VMEM((heads, 1, 128), F32),
        ],
        compiler_params=pltpu.CompilerParams(
            dimension_semantics=("arbitrary", "arbitrary", "arbitrary"), vmem_limit_bytes=VMEM_LIMIT),
        name="mlstm",
    )(proj, proj, proj, proj, gcol, grow, conv_w, conv_w, conv_b, conv_b, mnw)


def _sb_kernel(q_ref, k_ref, v_ref, z_ref, qnw_ref, knw_ref, ntri_ref,
               y_ref, kn_ref, vt_ref, acc_ref, carry_ref, z2_ref, *, tq, heads):
    qi = pl.program_id(2)
    hd = SB_HEAD_DIM
    tk = tq
    hs = [slice(h * hd, (h + 1) * hd) for h in range(heads)]
    nt_dims = (((1,), (1,)), ((), ()))

    @pl.when(qi == 0)
    def _():
        for h in range(heads):
            kf = k_ref[:, hs[h]].astype(F32)
            kn = kf * lax.rsqrt(jnp.mean(kf * kf, axis=-1, keepdims=True) + EPS) * knw_ref[...]
            kn_ref[:, hs[h]] = kn.astype(BF16)
            for j in range(vt_ref.shape[0]):
                vt_ref[j, hs[h], :] = v_ref[j * tk:(j + 1) * tk, hs[h]].astype(F32).T.astype(BF16)

    qbs = []
    for h in range(heads):
        qf = q_ref[:, hs[h]].astype(F32)
        qn = qf * lax.rsqrt(jnp.mean(qf * qf, axis=-1, keepdims=True) + EPS) * qnw_ref[...]
        qbs.append((qn * (hd ** -0.5 * LOG2E)).astype(BF16))
    ntri = ntri_ref[...]

    hr = range(heads)

    def scores(kj):
        start = pl.multiple_of(kj * tk, tk)
        return [lax.dot_general(kn_ref[pl.ds(start, tk), hs[h]], qbs[h], nt_dims,
                                preferred_element_type=F32) for h in hr]

    def softplus2(z2):
        return jnp.maximum(z2, 0.0) + jnp.log(1.0 + jnp.exp2(-jnp.abs(z2))) * LOG2E

    def suffix(sp2):
        return jnp.dot(ntri, sp2.astype(BF16), preferred_element_type=F32)

    def causal_mask():
        s_i = lax.broadcasted_iota(jnp.int32, (tk, tq), 0)
        t_i = lax.broadcasted_iota(jnp.int32, (tk, tq), 1)
        return s_i < t_i

    def prefetch_scores(kj, slot):
        z2n = scores(jnp.maximum(kj, 0))
        for h in hr:
            z2_ref[slot, h] = z2n[h]

    @pl.when(qi == 0)
    def _():
        causal = causal_mask()
        z2d = scores(qi)
        csd = [suffix(jnp.where(causal, softplus2(z2), 0.0)) for z2 in z2d]
        avd = [jnp.where(causal, jnp.exp2(z2d[h] + csd[h]), 0.0).astype(BF16) for h in hr]
        for h in hr:
            acc_ref[hs[h], :] = jnp.dot(vt_ref[qi, hs[h], :], avd[h], preferred_element_type=F32)
            carry_ref[h:h + 1, :] = csd[h][0:1, :]

    @pl.when(qi > 0)
    def _():
        causal = causal_mask()
        z2d = scores(qi)
        z2p = scores(qi - 1)
        spd = [jnp.where(causal, softplus2(z2), 0.0) for z2 in z2d]
        spp = [softplus2(z2) for z2 in z2p]
        csd = [suffix(sp2) for sp2 in spd]
        csp = [suffix(sp2) for sp2 in spp]
        prefetch_scores(qi - 2, 0)
        avd = [jnp.where(causal, jnp.exp2(z2d[h] + csd[h]), 0.0).astype(BF16) for h in hr]
        avp = [jnp.exp2(z2p[h] + csp[h] + csd[h][0:1, :]).astype(BF16) for h in hr]
        for h in hr:
            acc_ref[hs[h], :] = (jnp.dot(vt_ref[qi, hs[h], :], avd[h], preferred_element_type=F32)
                                 + jnp.dot(vt_ref[qi - 1, hs[h], :], avp[h], preferred_element_type=F32))
            carry_ref[h:h + 1, :] = csd[h][0:1, :] + csp[h][0:1, :]

    def still_alive():
        return (jnp.max(carry_ref[0:heads, :]) >= SB_DEAD_LOG2).astype(jnp.int32)

    def cond(state):
        return jnp.logical_and(state[0] < qi, state[1] > 0)

    def body(state):
        i = state[0]
        slot = lax.rem(i - 1, 2)
        kj = qi - 1 - i
        z2s = [z2_ref[slot, h] for h in hr]
        css = [suffix(softplus2(z2)) for z2 in z2s]
        prefetch_scores(kj - 1, 1 - slot)
        carries = [carry_ref[h:h + 1, :] for h in hr]
        avs = [jnp.exp2(z2s[h] + css[h] + carries[h]).astype(BF16) for h in hr]
        for h in hr:
            acc_ref[hs[h], :] += jnp.dot(vt_ref[kj, hs[h], :], avs[h], preferred_element_type=F32)
            carry_ref[h:h + 1, :] = carries[h] + css[h][0:1, :]
        return i + 1, still_alive()

    lax.while_loop(cond, body, (jnp.int32(1), still_alive()))

    for h in range(heads):
        zg = z_ref[:, hs[h]].astype(F32)
        y_ref[:, hs[h]] = (acc_ref[hs[h], :].T * _silu(zg, 1.0)).astype(BF16)


def _stickbreak(proj, qnw, knw, ntri, batch, seq, tq, heads):
    nq = seq // tq
    hw = heads * SB_HEAD_DIM
    return pl.pallas_call(
        functools.partial(_sb_kernel, tq=tq, heads=heads),
        grid=(batch, SB_HEADS // heads, nq),
        in_specs=[
            pl.BlockSpec((None, tq, hw), lambda b, h, q: (SEG_SQ, b * nq + q, h)),
            pl.BlockSpec((None, seq, hw), lambda b, h, q: (SEG_SK, b, h)),
            pl.BlockSpec((None, seq, hw), lambda b, h, q: (SEG_SV, b, h)),
            pl.BlockSpec((None, tq, hw), lambda b, h, q: (SEG_SZ, b * nq + q, h)),
            pl.BlockSpec((1, SB_HEAD_DIM), lambda b, h, q: (0, 0)),
            pl.BlockSpec((1, SB_HEAD_DIM), lambda b, h, q: (0, 0)),
            pl.BlockSpec((tq, tq), lambda b, h, q: (0, 0)),
        ],
        out_specs=pl.BlockSpec((tq, hw), lambda b, h, q: (b * nq + q, h)),
        out_shape=jax.ShapeDtypeStruct((batch * seq, SB_WIDTH), BF16),
        scratch_shapes=[
            pltpu.VMEM((seq, hw), BF16),
            pltpu.VMEM((seq // tq, hw, tq), BF16),
            pltpu.VMEM((hw, tq), F32),
            pltpu.VMEM((8, tq), F32),
            pltpu.VMEM((2, heads, tq, tq), F32),
        ],
        compiler_params=pltpu.CompilerParams(
            dimension_semantics=("arbitrary", "arbitrary", "arbitrary"), vmem_limit_bytes=VMEM_LIMIT),
        name="stickbreak",
    )(proj, proj, proj, proj, qnw, knw, ntri)


def _merge_kernel(x_ref, ym_ref, ys_ref, gm_ref, gs_ref, wm_ref, ws_ref, wo_ref, out_ref):
    pm = jnp.dot(ym_ref[...], wm_ref[...], preferred_element_type=F32)
    ps = jnp.dot(ys_ref[...], ws_ref[...], preferred_element_type=F32)
    merged = _sigmoid(gm_ref[...].astype(F32)) * pm + _sigmoid(gs_ref[...].astype(F32)) * ps
    out_ref[...] = x_ref[...] + jnp.dot(merged.astype(BF16), wo_ref[...], preferred_element_type=F32)


def _merge(x2, y_m, y_s, proj, wm, ws, wo, tm):
    m_rows = x2.shape[0]
    row = lambda i: (i, 0)
    const = lambda i: (0, 0)
    return pl.pallas_call(
        _merge_kernel,
        grid=(m_rows // tm,),
        in_specs=[
            pl.BlockSpec((tm, D_MODEL), row),
            pl.BlockSpec((tm, M_WIDTH), row),
            pl.BlockSpec((tm, SB_WIDTH), row),
            pl.BlockSpec((None, tm, SEG_W), lambda i: (SEG_GM, i, 0)),
            pl.BlockSpec((None, tm, SEG_W), lambda i: (SEG_GS, i, 0)),
            pl.BlockSpec((M_WIDTH, D_MODEL), const),
            pl.BlockSpec((SB_WIDTH, D_MODEL), const),
            pl.BlockSpec((D_MODEL, D_MODEL), const),
        ],
        out_specs=pl.BlockSpec((tm, D_MODEL), row),
        out_shape=jax.ShapeDtypeStruct((m_rows, D_MODEL), F32),
        compiler_params=pltpu.CompilerParams(
            dimension_semantics=("arbitrary",), vmem_limit_bytes=VMEM_LIMIT),
        name="merge",
    )(x2, y_m, y_s, proj, proj, wm, ws, wo)


def _split_weights(w_in, b_in):
    mw = M_WIDTH
    g0 = 3 * mw
    g1 = g0 + 2 * M_HEADS
    w_main = jnp.concatenate([w_in[:, :g0], w_in[:, g1:]], axis=1)
    b_main = jnp.concatenate([b_in[:g0], b_in[g1:]])[None, :]
    w_gate = w_in[:, g0:g1]
    b_gate = b_in[g0:g1]
    wg = jnp.pad(w_gate, ((0, 0), (0, GATE_PAD - 2 * M_HEADS)))
    bg = jnp.pad(b_gate, (0, GATE_PAD - 2 * M_HEADS))[None, :]
    wgt = jnp.pad(w_gate.T, ((0, GATE_ROWS - 2 * M_HEADS), (0, 0)))
    bgt = b_gate[:, None]
    return w_main.astype(BF16), b_main, wg.astype(BF16), bg, wgt.astype(BF16), bgt


def kernel(x, norm_w, w_in, b_in, conv_w, conv_b, mlstm_norm_w, sb_q_norm_w, sb_k_norm_w,
           w_proj_m, w_proj_s, w_out):
    batch, seq, d_model = x.shape
    assert d_model == D_MODEL and seq % 256 == 0
    m_rows = batch * seq
    x2 = x.reshape(m_rows, d_model)

    tm = min(1024, seq)
    ts = min(512, seq)
    tq = 256
    assert seq % tm == 0 and seq % ts == 0

    w_main, b_main, wg, bg, wgt, bgt = _split_weights(w_in, b_in)
    proj, gcol, grow = _inproj(x2, norm_w[None, :], w_main, b_main, wg, bg, wgt, bgt, tm)

    y_m = _mlstm(proj, gcol, grow, conv_w, conv_b[None, :], mlstm_norm_w[None, :], batch, seq, ts, M_GROUP)

    ntri = -(jnp.arange(tq)[:, None] <= jnp.arange(tq)[None, :]).astype(BF16)
    y_s = _stickbreak(proj, sb_q_norm_w[None, :], sb_k_norm_w[None, :], ntri, batch, seq, tq, SB_GROUP)

    out = _merge(x2, y_m, y_s, proj, w_proj_m.astype(BF16), w_proj_s.astype(BF16),
                 w_out.astype(BF16), tm)
    return out.reshape(batch, seq, d_model)
```

```python
import functools

import jax
import jax.numpy as jnp
from jax import lax
from jax.experimental import pallas as pl
from jax.experimental.pallas import tpu as pltpu

F32 = jnp.float32
BF16 = jnp.bfloat16

EPS = 1e-6
D_MODEL = 1024
CHUNK = 64
M_HEADS = 4
M_HEAD_DIM = 256
M_WIDTH = M_HEADS * M_HEAD_DIM
SB_HEADS = 8
SB_HEAD_DIM = 128
SB_WIDTH = SB_HEADS * SB_HEAD_DIM
CONV_WIDTH = 4
N_BRANCH = 2

IN_MQ, IN_MK, IN_MV, IN_MO, IN_MZ, IN_SQ, IN_SK, IN_SV, IN_SZ, IN_GM, IN_GS = range(11)
N_IN = 11
SEG_MQ, SEG_MK, SEG_MV, SEG_MG, SEG_SQ, SEG_SK, SEG_SV, SEG_SZ, SEG_GM, SEG_GS = range(10)
N_SEG = 10
SEG_W = 1024
GATE_PAD = 128
GATE_ROWS = 16
CONV_PHASES = 4
CONV_HALO = 8
M_GROUP = 4
SB_GROUP = 4
LOG2E = 1.4426950408889634
SB_DEAD_LOG2 = -160.0

VMEM_LIMIT = 56 * 1024 * 1024


def _logsig(x):
    return jnp.minimum(x, 0.0) - jnp.log1p(jnp.exp(-jnp.abs(x)))


def _sigmoid(x):
    return 0.5 + 0.5 * jnp.tanh(0.5 * x)


def _silu(x, scale):
    hx = x * (0.5 * scale)
    return hx + hx * jnp.tanh(0.5 * x)


def _inproj_kernel(x_ref, nw_ref, w_ref, b_ref, wg_ref, bg_ref, wgt_ref, bgt_ref,
                   proj_ref, gcol_ref, grow_ref, h_ref, og_ref):
    j = pl.program_id(1)

    @pl.when(j == 0)
    def _():
        x = x_ref[...]
        y = x * lax.rsqrt(jnp.mean(x * x, axis=-1, keepdims=True) + EPS) * nw_ref[...]
        hb = y.astype(BF16)
        h_ref[...] = hb
        gcol_ref[...] = jnp.dot(hb, wg_ref[...], preferred_element_type=F32) + bg_ref[...]
        gt = lax.dot_general(wgt_ref[...], hb, (((1,), (1,)), ((), ())), preferred_element_type=F32)
        grow_ref[...] = gt[0:8, :] + bgt_ref[...]

    def project():
        return jnp.dot(h_ref[...], w_ref[...], preferred_element_type=F32) + b_ref[...]

    @pl.when(j == IN_MO)
    def _():
        og_ref[...] = _sigmoid(project())

    @pl.when(j == IN_MZ)
    def _():
        proj_ref[...] = (og_ref[...] * _silu(project(), 1.0)).astype(BF16)

    @pl.when(jnp.logical_and(j != IN_MO, j != IN_MZ))
    def _():
        proj_ref[...] = project().astype(BF16)


def _inproj(x2, norm_w, w_main, b_main, wg, bg, wgt, bgt, tm):
    m_rows = x2.shape[0]
    grid = (m_rows // tm, N_IN)
    return pl.pallas_call(
        _inproj_kernel,
        grid=grid,
        in_specs=[
            pl.BlockSpec((tm, D_MODEL), lambda i, j: (i, 0)),
            pl.BlockSpec((1, D_MODEL), lambda i, j: (0, 0)),
            pl.BlockSpec((D_MODEL, SEG_W), lambda i, j: (0, j)),
            pl.BlockSpec((1, SEG_W), lambda i, j: (0, j)),
            pl.BlockSpec((D_MODEL, GATE_PAD), lambda i, j: (0, 0)),
            pl.BlockSpec((1, GATE_PAD), lambda i, j: (0, 0)),
            pl.BlockSpec((GATE_ROWS, D_MODEL), lambda i, j: (0, 0)),
            pl.BlockSpec((8, 1), lambda i, j: (0, 0)),
        ],
        out_specs=[
            pl.BlockSpec((None, tm, SEG_W), lambda i, j: (j - (j > IN_MO).astype(jnp.int32), i, 0)),
            pl.BlockSpec((tm, GATE_PAD), lambda i, j: (i, 0)),
            pl.BlockSpec((8, tm), lambda i, j: (0, i)),
        ],
        out_shape=[
            jax.ShapeDtypeStruct((N_SEG, m_rows, SEG_W), BF16),
            jax.ShapeDtypeStruct((m_rows, GATE_PAD), F32),
            jax.ShapeDtypeStruct((8, m_rows), F32),
        ],
        scratch_shapes=[
            pltpu.VMEM((tm, D_MODEL), BF16),
            pltpu.VMEM((tm, SEG_W), F32),
        ],
        compiler_params=pltpu.CompilerParams(
            dimension_semantics=("arbitrary", "arbitrary"), vmem_limit_bytes=VMEM_LIMIT),
        name="inproj",
    )(x2, norm_w, w_main, b_main, wg, bg, wgt, bgt)


def _mlstm_kernel(qp_ref, kp_ref, v_ref, og_ref, gcol_ref, grow_ref, cwq_ref, cwk_ref, cbq_ref, cbk_ref, nw_ref,
                  y_ref, qext, kext, cq_ref, ck_ref, q_ref, k_ref, hbuf, c_ref, n_ref, m_ref, *, ts, heads):
    head0 = pl.program_id(1) * heads
    t_idx = pl.program_id(2)
    hd = M_HEAD_DIM
    nchunk = ts // CHUNK
    hr = range(heads)
    cols = [slice(h * hd, (h + 1) * hd) for h in hr]
    rows = [slice(c * CHUNK, (c + 1) * CHUNK) for c in range(nchunk)]
    units = [(h, c) for c in range(nchunk) for h in hr]

    nslab = qext.shape[0]
    lanes = [slice(c * 128, (c + 1) * 128) for c in range(nslab)]

    @pl.when(t_idx == 0)
    def _():
        qext[:, 0:CONV_HALO, :] = jnp.zeros((nslab, CONV_HALO, 128), F32)
        kext[:, 0:CONV_HALO, :] = jnp.zeros((nslab, CONV_HALO, 128), F32)
        c_ref[...] = jnp.zeros_like(c_ref)
        n_ref[...] = jnp.zeros_like(n_ref)
        m_ref[...] = jnp.zeros_like(m_ref)

    @pl.when(t_idx > 0)
    def _():
        qext[:, 0:CONV_HALO, :] = qext[:, ts:ts + CONV_HALO, :]
        kext[:, 0:CONV_HALO, :] = kext[:, ts:ts + CONV_HALO, :]

    for c in range(nslab):
        qext[c, CONV_HALO:CONV_HALO + ts, :] = qp_ref[:, lanes[c]].astype(F32)
        kext[c, CONV_HALO:CONV_HALO + ts, :] = kp_ref[:, lanes[c]].astype(F32)

    def conv_silu(ext, out, dst, cw_ref, cb_ref, scale):
        n = ts // CONV_PHASES
        for c in range(nslab):
            for p in range(CONV_PHASES):
                acc = cb_ref[:, lanes[c]]
                for tap in range(CONV_WIDTH):
                    start = CONV_HALO - (CONV_WIDTH - 1) + tap + p
                    acc = acc + cw_ref[tap:tap + 1, lanes[c]] * ext[c, pl.ds(start, n, stride=CONV_PHASES), :]
                out[c, pl.ds(p, n, stride=CONV_PHASES), :] = _silu(acc, scale)
        for c in range(nslab):
            dst[:, lanes[c]] = out[c].astype(BF16)

    conv_silu(qext, cq_ref, q_ref, cwq_ref, cbq_ref, 1.0)
    conv_silu(kext, ck_ref, k_ref, cwk_ref, cbk_ref, hd ** -0.5)

    g = gcol_ref[...]
    lane = lax.broadcasted_iota(jnp.int32, g.shape, 1)
    icol_all = [jnp.sum(jnp.where(lane == head0 + h, g, 0.0), axis=-1, keepdims=True) for h in hr]
    fcol_all = [_logsig(jnp.sum(jnp.where(lane == head0 + h + M_HEADS, g, 0.0), axis=-1, keepdims=True))
                for h in hr]
    irow_all = [grow_ref[pl.ds(head0 + h, 1), :] for h in hr]
    frow_all = [_logsig(grow_ref[pl.ds(head0 + h + M_HEADS, 1), :]) for h in hr]

    r_i = lax.broadcasted_iota(jnp.int32, (CHUNK, CHUNK), 0)
    c_i = lax.broadcasted_iota(jnp.int32, (CHUNK, CHUNK), 1)
    tril = r_i >= c_i

    qb = {u: q_ref[rows[u[1]], cols[u[0]]] for u in units}
    vb = {u: v_ref[rows[u[1]], cols[u[0]]] for u in units}
    qk = {u: lax.dot_general(qb[u], k_ref[rows[u[1]], cols[u[0]]], (((1,), (1,)), ((), ())),
                             preferred_element_type=F32) for u in units}
    bcum = {u: jnp.sum(jnp.where(tril, frow_all[u[0]][:, rows[u[1]]], 0.0), axis=-1, keepdims=True)
            for u in units}
    bcum_row = {u: jnp.sum(jnp.where(r_i <= c_i, fcol_all[u[0]][rows[u[1]], :], 0.0), axis=0, keepdims=True)
                for u in units}
    dmat = {u: jnp.where(tril, bcum[u] - bcum_row[u] + irow_all[u[0]][:, rows[u[1]]], -jnp.inf) for u in units}
    mx = {u: jnp.max(dmat[u], axis=-1, keepdims=True) for u in units}
    b_last = {u: bcum[u][CHUNK - 1:CHUNK, :] for u in units}
    decay = {u: b_last[u] - bcum[u] + icol_all[u[0]][rows[u[1]], :] for u in units}
    mloc = {u: jnp.max(decay[u], axis=0, keepdims=True) for u in units}
    kwl = {u: k_ref[rows[u[1]], cols[u[0]]].astype(F32) * jnp.exp(decay[u] - mloc[u]) for u in units}
    upd = {u: lax.dot_general(kwl[u].astype(BF16), vb[u], (((0,), (0,)), ((), ())),
                              preferred_element_type=F32) for u in units}
    ksum = {u: jnp.sum(kwl[u], axis=0, keepdims=True) for u in units}
    sl = {u: qk[u] * jnp.exp(dmat[u] - mx[u]) for u in units}
    rs = {u: jnp.sum(sl[u], axis=-1, keepdims=True) for u in units}
    sv = {u: jnp.dot(sl[u].astype(BF16), vb[u], preferred_element_type=F32) for u in units}

    m_prev = [m_ref[h][:, 0:1] for h in hr]
    cmat = [c_ref[h] for h in hr]
    nvec = [n_ref[h] for h in hr]
    inter, q_state, q_n = {}, {}, {}
    for u in units:
        h = u[0]
        inter[u] = bcum[u] + m_prev[h]
        q_state[u] = jnp.dot(qb[u], cmat[h].astype(BF16), preferred_element_type=F32)
        q_n[u] = jnp.sum(qb[u].astype(F32) * nvec[h], axis=-1, keepdims=True)
        m_new = jnp.maximum(b_last[u] + m_prev[h], mloc[u])
        keep = jnp.exp(b_last[u] + m_prev[h] - m_new)
        add = jnp.exp(mloc[u] - m_new)
        cmat[h] = keep * cmat[h] + add * upd[u]
        nvec[h] = keep * nvec[h] + add * ksum[u]
        m_prev[h] = m_new
    for h in hr:
        c_ref[h] = cmat[h]
        n_ref[h] = nvec[h]
        m_ref[h] = jnp.broadcast_to(m_prev[h], m_ref.shape[1:])

    m_t = {u: jnp.maximum(inter[u], mx[u]) for u in units}
    r_intra = {u: jnp.exp(mx[u] - m_t[u]) for u in units}
    r_inter = {u: jnp.exp(inter[u] - m_t[u]) for u in units}
    inv = {u: 1.0 / jnp.maximum(jnp.abs(r_intra[u] * rs[u] + r_inter[u] * q_n[u]), jnp.exp(-m_t[u]))
           for u in units}
    w_sv = {u: r_intra[u] * inv[u] for u in units}
    w_state = {u: r_inter[u] * inv[u] for u in units}
    for u in units:
        hbuf[rows[u[1]], cols[u[0]]] = w_sv[u] * sv[u] + w_state[u] * q_state[u]

    for h in hr:
        hm = hbuf[:, cols[h]]
        hn = hm * lax.rsqrt(jnp.mean(hm * hm, axis=-1, keepdims=True) + EPS) * nw_ref[:, cols[h]]
        y_ref[:, cols[h]] = (hn * og_ref[:, cols[h]].astype(F32)).astype(BF16)


def _mlstm(proj, gcol, grow, conv_w, conv_b, mnw, batch, seq, ts, heads):
    nt = seq // ts
    hw = heads * M_HEAD_DIM

    def seg_spec(seg):
        return pl.BlockSpec((None, ts, hw), lambda b, h, t: (seg, b * nt + t, h))

    return pl.pallas_call(
        functools.partial(_mlstm_kernel, ts=ts, heads=heads),
        grid=(batch, M_HEADS // heads, nt),
        in_specs=[
            seg_spec(SEG_MQ), seg_spec(SEG_MK), seg_spec(SEG_MV), seg_spec(SEG_MG),
            pl.BlockSpec((ts, GATE_PAD), lambda b, h, t: (b * nt + t, 0)),
            pl.BlockSpec((8, ts), lambda b, h, t: (0, b * nt + t)),
            pl.BlockSpec((CONV_WIDTH, hw), lambda b, h, t: (0, h)),
            pl.BlockSpec((CONV_WIDTH, hw), lambda b, h, t: (0, M_HEADS // heads + h)),
            pl.BlockSpec((1, hw), lambda b, h, t: (0, h)),
            pl.BlockSpec((1, hw), lambda b, h, t: (0, M_HEADS // heads + h)),
            pl.BlockSpec((1, hw), lambda b, h, t: (0, h)),
        ],
        out_specs=pl.BlockSpec((ts, hw), lambda b, h, t: (b * nt + t, h)),
        out_shape=jax.ShapeDtypeStruct((batch * seq, M_WIDTH), BF16),
        scratch_shapes=[
            pltpu.VMEM((hw // 128, ts + CONV_HALO, 128), F32),
            pltpu.VMEM((hw // 128, ts + CONV_HALO, 128), F32),
            pltpu.VMEM((hw // 128, ts, 128), F32),
            pltpu.VMEM((hw // 128, ts, 128), F32),
            pltpu.VMEM((ts, hw), BF16),
            pltpu.VMEM((ts, hw), BF16),
            pltpu.VMEM((ts, hw), F32),
            pltpu.VMEM((heads, M_HEAD_DIM, M_HEAD_DIM), F32),
            pltpu.VMEM((heads, 1, M_HEAD_DIM), F32),
            pltpu.VMEM((heads, 1, 128), F32),
        ],
        compiler_params=pltpu.CompilerParams(
            dimension_semantics=("arbitrary", "arbitrary", "arbitrary"), vmem_limit_bytes=VMEM_LIMIT),
        name="mlstm",
    )(proj, proj, proj, proj, gcol, grow, conv_w, conv_w, conv_b, conv_b, mnw)


def _sb_kernel(q_ref, k_ref, v_ref, z_ref, qnw_ref, knw_ref, ntri_ref,
               y_ref, kn_ref, vt_ref, acc_ref, carry_ref, z2_ref, *, tq, heads):
    qi = pl.program_id(2)
    hd = SB_HEAD_DIM
    tk = tq
    hs = [slice(h * hd, (h + 1) * hd) for h in range(heads)]
    nt_dims = (((1,), (1,)), ((), ()))

    @pl.when(qi == 0)
    def _():
        for h in range(heads):
            kf = k_ref[:, hs[h]].astype(F32)
            kn = kf * lax.rsqrt(jnp.mean(kf * kf, axis=-1, keepdims=True) + EPS) * knw_ref[...]
            kn_ref[:, hs[h]] = kn.astype(BF16)
            for j in range(vt_ref.shape[0]):
                vt_ref[j, hs[h], :] = v_ref[j * tk:(j + 1) * tk, hs[h]].astype(F32).T.astype(BF16)

    qbs = []
    for h in range(heads):
        qf = q_ref[:, hs[h]].astype(F32)
        qn = qf * lax.rsqrt(jnp.mean(qf * qf, axis=-1, keepdims=True) + EPS) * qnw_ref[...]
        qbs.append((qn * (hd ** -0.5 * LOG2E)).astype(BF16))
    ntri = ntri_ref[...]

    hr = range(heads)

    def scores(kj):
        start = pl.multiple_of(kj * tk, tk)
        return [lax.dot_general(kn_ref[pl.ds(start, tk), hs[h]], qbs[h], nt_dims,
                                preferred_element_type=F32) for h in hr]

    def softplus2(z2):
        return jnp.maximum(z2, 0.0) + jnp.log(1.0 + jnp.exp2(-jnp.abs(z2))) * LOG2E

    def suffix(sp2):
        return jnp.dot(ntri, sp2.astype(BF16), preferred_element_type=F32)

    def causal_mask():
        s_i = lax.broadcasted_iota(jnp.int32, (tk, tq), 0)
        t_i = lax.broadcasted_iota(jnp.int32, (tk, tq), 1)
        return s_i < t_i

    def prefetch_scores(kj, slot):
        z2n = scores(jnp.maximum(kj, 0))
        for h in hr:
            z2_ref[slot, h] = z2n[h]

    @pl.when(qi == 0)
    def _():
        causal = causal_mask()
        z2d = scores(qi)
        csd = [suffix(jnp.where(causal, softplus2(z2), 0.0)) for z2 in z2d]
        avd = [jnp.where(causal, jnp.exp2(z2d[h] + csd[h]), 0.0).astype(BF16) for h in hr]
        for h in hr:
            acc_ref[hs[h], :] = jnp.dot(vt_ref[qi, hs[h], :], avd[h], preferred_element_type=F32)
            carry_ref[h:h + 1, :] = csd[h][0:1, :]

    @pl.when(qi > 0)
    def _():
        causal = causal_mask()
        z2d = scores(qi)
        z2p = scores(qi - 1)
        spd = [jnp.where(causal, softplus2(z2), 0.0) for z2 in z2d]
        spp = [softplus2(z2) for z2 in z2p]
        csd = [suffix(sp2) for sp2 in spd]
        csp = [suffix(sp2) for sp2 in spp]
        prefetch_scores(qi - 2, 0)
        avd = [jnp.where(causal, jnp.exp2(z2d[h] + csd[h]), 0.0).astype(BF16) for h in hr]
        avp = [jnp.exp2(z2p[h] + csp[h] + csd[h][0:1, :]).astype(BF16) for h in hr]
        for h in hr:
            acc_ref[hs[h], :] = (jnp.dot(vt_ref[qi, hs[h], :], avd[h], preferred_element_type=F32)
                                 + jnp.dot(vt_ref[qi - 1, hs[h], :], avp[h], preferred_element_type=F32))
            carry_ref[h:h + 1, :] = csd[h][0:1, :] + csp[h][0:1, :]

    def still_alive():
        return (jnp.max(carry_ref[0:heads, :]) >= SB_DEAD_LOG2).astype(jnp.int32)

    def cond(state):
        return jnp.logical_and(state[0] < qi, state[1] > 0)

    def body(state):
        i = state[0]
        slot = lax.rem(i - 1, 2)
        kj = qi - 1 - i
        z2s = [z2_ref[slot, h] for h in hr]
        css = [suffix(softplus2(z2)) for z2 in z2s]
        prefetch_scores(kj - 1, 1 - slot)
        carries = [carry_ref[h:h + 1, :] for h in hr]
        avs = [jnp.exp2(z2s[h] + css[h] + carries[h]).astype(BF16) for h in hr]
        for h in hr:
            acc_ref[hs[h], :] += jnp.dot(vt_ref[kj, hs[h], :], avs[h], preferred_element_type=F32)
            carry_ref[h:h + 1, :] = carries[h] + css[h][0:1, :]
        return i + 1, still_alive()

    lax.while_loop(cond, body, (jnp.int32(1), still_alive()))

    for h in range(heads):
        zg = z_ref[:, hs[h]].astype(F32)
        y_ref[:, hs[h]] = (acc_ref[hs[h], :].T * _silu(zg, 1.0)).astype(BF16)


def _stickbreak(proj, qnw, knw, ntri, batch, seq, tq, heads):
    nq = seq // tq
    hw = heads * SB_HEAD_DIM
    return pl.pallas_call(
        functools.partial(_sb_kernel, tq=tq, heads=heads),
        grid=(batch, SB_HEADS // heads, nq),
        in_specs=[
            pl.BlockSpec((None, tq, hw), lambda b, h, q: (SEG_SQ, b * nq + q, h)),
            pl.BlockSpec((None, seq, hw), lambda b, h, q: (SEG_SK, b, h)),
            pl.BlockSpec((None, seq, hw), lambda b, h, q: (SEG_SV, b, h)),
            pl.BlockSpec((None, tq, hw), lambda b, h, q: (SEG_SZ, b * nq + q, h)),
            pl.BlockSpec((1, SB_HEAD_DIM), lambda b, h, q: (0, 0)),
            pl.BlockSpec((1, SB_HEAD_DIM), lambda b, h, q: (0, 0)),
            pl.BlockSpec((tq, tq), lambda b, h, q: (0, 0)),
        ],
        out_specs=pl.BlockSpec((tq, hw), lambda b, h, q: (b * nq + q, h)),
        out_shape=jax.ShapeDtypeStruct((batch * seq, SB_WIDTH), BF16),
        scratch_shapes=[
            pltpu.VMEM((seq, hw), BF16),
            pltpu.VMEM((seq // tq, hw, tq), BF16),
            pltpu.VMEM((hw, tq), F32),
            pltpu.VMEM((8, tq), F32),
            pltpu.VMEM((2, heads, tq, tq), F32),
        ],
        compiler_params=pltpu.CompilerParams(
            dimension_semantics=("arbitrary", "arbitrary", "arbitrary"), vmem_limit_bytes=VMEM_LIMIT),
        name="stickbreak",
    )(proj, proj, proj, proj, qnw, knw, ntri)


def _merge_kernel(x_ref, ym_ref, ys_ref, gm_ref, gs_ref, wm_ref, ws_ref, wo_ref, out_ref):
    pm = jnp.dot(ym_ref[...], wm_ref[...], preferred_element_type=F32)
    ps = jnp.dot(ys_ref[...], ws_ref[...], preferred_element_type=F32)
    merged = _sigmoid(gm_ref[...].astype(F32)) * pm + _sigmoid(gs_ref[...].astype(F32)) * ps
    out_ref[...] = x_ref[...] + jnp.dot(merged.astype(BF16), wo_ref[...], preferred_element_type=F32)


def _merge(x2, y_m, y_s, proj, wm, ws, wo, tm):
    m_rows = x2.shape[0]
    row = lambda i: (i, 0)
    const = lambda i: (0, 0)
    return pl.pallas_call(
        _merge_kernel,
        grid=(m_rows // tm,),
        in_specs=[
            pl.BlockSpec((tm, D_MODEL), row),
            pl.BlockSpec((tm, M_WIDTH), row),
            pl.BlockSpec((tm, SB_WIDTH), row),
            pl.BlockSpec((None, tm, SEG_W), lambda i: (SEG_GM, i, 0)),
            pl.BlockSpec((None, tm, SEG_W), lambda i: (SEG_GS, i, 0)),
            pl.BlockSpec((M_WIDTH, D_MODEL), const),
            pl.BlockSpec((SB_WIDTH, D_MODEL), const),
            pl.BlockSpec((D_MODEL, D_MODEL), const),
        ],
        out_specs=pl.BlockSpec((tm, D_MODEL), row),
        out_shape=jax.ShapeDtypeStruct((m_rows, D_MODEL), F32),
        compiler_params=pltpu.CompilerParams(
            dimension_semantics=("arbitrary",), vmem_limit_bytes=VMEM_LIMIT),
        name="merge",
    )(x2, y_m, y_s, proj, proj, wm, ws, wo)


def _split_weights(w_in, b_in):
    mw = M_WIDTH
    g0 = 3 * mw
    g1 = g0 + 2 * M_HEADS
    w_main = jnp.concatenate([w_in[:, :g0], w_in[:, g1:]], axis=1)
    b_main = jnp.concatenate([b_in[:g0], b_in[g1:]])[None, :]
    w_gate = w_in[:, g0:g1]
    b_gate = b_in[g0:g1]
    wg = jnp.pad(w_gate, ((0, 0), (0, GATE_PAD - 2 * M_HEADS)))
    bg = jnp.pad(b_gate, (0, GATE_PAD - 2 * M_HEADS))[None, :]
    wgt = jnp.pad(w_gate.T, ((0, GATE_ROWS - 2 * M_HEADS), (0, 0)))
    bgt = b_gate[:, None]
    return w_main.astype(BF16), b_main, wg.astype(BF16), bg, wgt.astype(BF16), bgt


def kernel(x, norm_w, w_in, b_in, conv_w, conv_b, mlstm_norm_w, sb_q_norm_w, sb_k_norm_w,
           w_proj_m, w_proj_s, w_out):
    batch, seq, d_model = x.shape
    assert d_model == D_MODEL and seq % 256 == 0
    m_rows = batch * seq
    x2 = x.reshape(m_rows, d_model)

    tm = min(1024, seq)
    ts = min(512, seq)
    tq = 256
    assert seq % tm == 0 and seq % ts == 0

    w_main, b_main, wg, bg, wgt, bgt = _split_weights(w_in, b_in)
    proj, gcol, grow = _inproj(x2, norm_w[None, :], w_main, b_main, wg, bg, wgt, bgt, tm)

    y_m = _mlstm(proj, gcol, grow, conv_w, conv_b[None, :], mlstm_norm_w[None, :], batch, seq, ts, M_GROUP)

    ntri = -(jnp.arange(tq)[:, None] <= jnp.arange(tq)[None, :]).astype(BF16)
    y_s = _stickbreak(proj, sb_q_norm_w[None, :], sb_k_norm_w[None, :], ntri, batch, seq, tq, SB_GROUP)

    out = _merge(x2, y_m, y_s, proj, w_proj_m.astype(BF16), w_proj_s.astype(BF16),
                 w_out.astype(BF16), tm)
    return out.reshape(batch, seq, d_model)
```

```python
import functools

import jax
import jax.numpy as jnp
from jax import lax
from jax.experimental import pallas as pl
from jax.experimental.pallas import tpu as pltpu

F32 = jnp.float32
BF16 = jnp.bfloat16

EPS = 1e-6
D_MODEL = 1024
CHUNK = 64
M_HEADS = 4
M_HEAD_DIM = 256
M_WIDTH = M_HEADS * M_HEAD_DIM
SB_HEADS = 8
SB_HEAD_DIM = 128
SB_WIDTH = SB_HEADS * SB_HEAD_DIM
CONV_WIDTH = 4
N_BRANCH = 2

IN_MQ, IN_MK, IN_MV, IN_MO, IN_MZ, IN_SQ, IN_SK, IN_SV, IN_SZ, IN_GM, IN_GS = range(11)
N_IN = 11
SEG_MQ, SEG_MK, SEG_MV, SEG_MG, SEG_SQ, SEG_SK, SEG_SV, SEG_SZ, SEG_GM, SEG_GS = range(10)
N_SEG = 10
SEG_W = 1024
GATE_PAD = 128
CONV_PHASES = 4
CONV_HALO = 8
M_GROUP = 4
SB_GROUP = 4
LOG2E = 1.4426950408889634
SB_DEAD_LOG2 = -160.0

VMEM_LIMIT = 56 * 1024 * 1024


def _logsig(x):
    return jnp.minimum(x, 0.0) - jnp.log1p(jnp.exp(-jnp.abs(x)))


def _sigmoid(x):
    return 0.5 + 0.5 * jnp.tanh(0.5 * x)


def _silu(x, scale):
    hx = x * (0.5 * scale)
    return hx + hx * jnp.tanh(0.5 * x)


def _inproj_kernel(x_ref, nw_ref, w_ref, b_ref, wg_ref, bg_ref,
                   proj_ref, gcol_ref, grow_ref, h_ref, og_ref):
    j = pl.program_id(1)

    @pl.when(j == 0)
    def _():
        x = x_ref[...]
        y = x * lax.rsqrt(jnp.mean(x * x, axis=-1, keepdims=True) + EPS) * nw_ref[...]
        hb = y.astype(BF16)
        h_ref[...] = hb
        gcol = jnp.dot(hb, wg_ref[...], preferred_element_type=F32) + bg_ref[...]
        gcol_ref[...] = gcol
        grow_ref[...] = gcol.T[0:8, :]

    def project():
        return jnp.dot(h_ref[...], w_ref[...], preferred_element_type=F32) + b_ref[...]

    @pl.when(j == IN_MO)
    def _():
        og_ref[...] = _sigmoid(project())

    @pl.when(j == IN_MZ)
    def _():
        proj_ref[...] = (og_ref[...] * _silu(project(), 1.0)).astype(BF16)

    @pl.when(jnp.logical_and(j != IN_MO, j != IN_MZ))
    def _():
        proj_ref[...] = project().astype(BF16)


def _inproj(x2, norm_w, w_main, b_main, wg, bg, tm):
    m_rows = x2.shape[0]
    grid = (m_rows // tm, N_IN)
    return pl.pallas_call(
        _inproj_kernel,
        grid=grid,
        in_specs=[
            pl.BlockSpec((tm, D_MODEL), lambda i, j: (i, 0)),
            pl.BlockSpec((1, D_MODEL), lambda i, j: (0, 0)),
            pl.BlockSpec((D_MODEL, SEG_W), lambda i, j: (0, j)),
            pl.BlockSpec((1, SEG_W), lambda i, j: (0, j)),
            pl.BlockSpec((D_MODEL, GATE_PAD), lambda i, j: (0, 0)),
            pl.BlockSpec((1, GATE_PAD), lambda i, j: (0, 0)),
        ],
        out_specs=[
            pl.BlockSpec((None, tm, SEG_W), lambda i, j: (j - (j > IN_MO).astype(jnp.int32), i, 0)),
            pl.BlockSpec((tm, GATE_PAD), lambda i, j: (i, 0)),
            pl.BlockSpec((8, tm), lambda i, j: (0, i)),
        ],
        out_shape=[
            jax.ShapeDtypeStruct((N_SEG, m_rows, SEG_W), BF16),
            jax.ShapeDtypeStruct((m_rows, GATE_PAD), F32),
            jax.ShapeDtypeStruct((8, m_rows), F32),
        ],
        scratch_shapes=[
            pltpu.VMEM((tm, D_MODEL), BF16),
            pltpu.VMEM((tm, SEG_W), F32),
        ],
        compiler_params=pltpu.CompilerParams(
            dimension_semantics=("arbitrary", "arbitrary"), vmem_limit_bytes=VMEM_LIMIT),
        name="inproj",
    )(x2, norm_w, w_main, b_main, wg, bg)


def _mlstm_kernel(qp_ref, kp_ref, v_ref, og_ref, gcol_ref, grow_ref, cwq_ref, cwk_ref, cbq_ref, cbk_ref, nw_ref,
                  y_ref, qext, kext, cq_ref, ck_ref, q_ref, k_ref, hbuf, c_ref, n_ref, m_ref, *, ts, heads):
    head0 = pl.program_id(1) * heads
    t_idx = pl.program_id(2)
    hd = M_HEAD_DIM
    nchunk = ts // CHUNK
    hr = range(heads)
    cols = [slice(h * hd, (h + 1) * hd) for h in hr]
    rows = [slice(c * CHUNK, (c + 1) * CHUNK) for c in range(nchunk)]
    units = [(h, c) for c in range(nchunk) for h in hr]

    nslab = qext.shape[0]
    lanes = [slice(c * 128, (c + 1) * 128) for c in range(nslab)]

    @pl.when(t_idx == 0)
    def _():
        qext[:, 0:CONV_HALO, :] = jnp.zeros((nslab, CONV_HALO, 128), F32)
        kext[:, 0:CONV_HALO, :] = jnp.zeros((nslab, CONV_HALO, 128), F32)
        c_ref[...] = jnp.zeros_like(c_ref)
        n_ref[...] = jnp.zeros_like(n_ref)
        m_ref[...] = jnp.zeros_like(m_ref)

    @pl.when(t_idx > 0)
    def _():
        qext[:, 0:CONV_HALO, :] = qext[:, ts:ts + CONV_HALO, :]
        kext[:, 0:CONV_HALO, :] = kext[:, ts:ts + CONV_HALO, :]

    for c in range(nslab):
        qext[c, CONV_HALO:CONV_HALO + ts, :] = qp_ref[:, lanes[c]].astype(F32)
        kext[c, CONV_HALO:CONV_HALO + ts, :] = kp_ref[:, lanes[c]].astype(F32)

    def conv_silu(ext, out, dst, cw_ref, cb_ref, scale):
        n = ts // CONV_PHASES
        for c in range(nslab):
            for p in range(CONV_PHASES):
                acc = cb_ref[:, lanes[c]]
                for tap in range(CONV_WIDTH):
                    start = CONV_HALO - (CONV_WIDTH - 1) + tap + p
                    acc = acc + cw_ref[tap:tap + 1, lanes[c]] * ext[c, pl.ds(start, n, stride=CONV_PHASES), :]
                out[c, pl.ds(p, n, stride=CONV_PHASES), :] = _silu(acc, scale)
        for c in range(nslab):
            dst[:, lanes[c]] = out[c].astype(BF16)

    conv_silu(qext, cq_ref, q_ref, cwq_ref, cbq_ref, 1.0)
    conv_silu(kext, ck_ref, k_ref, cwk_ref, cbk_ref, hd ** -0.5)

    g = gcol_ref[...]
    lane = lax.broadcasted_iota(jnp.int32, g.shape, 1)
    icol_all = [jnp.sum(jnp.where(lane == head0 + h, g, 0.0), axis=-1, keepdims=True) for h in hr]
    fcol_all = [_logsig(jnp.sum(jnp.where(lane == head0 + h + M_HEADS, g, 0.0), axis=-1, keepdims=True))
                for h in hr]
    irow_all = [grow_ref[pl.ds(head0 + h, 1), :] for h in hr]
    frow_all = [_logsig(grow_ref[pl.ds(head0 + h + M_HEADS, 1), :]) for h in hr]

    r_i = lax.broadcasted_iota(jnp.int32, (CHUNK, CHUNK), 0)
    c_i = lax.broadcasted_iota(jnp.int32, (CHUNK, CHUNK), 1)
    tril = r_i >= c_i

    qb = {u: q_ref[rows[u[1]], cols[u[0]]] for u in units}
    vb = {u: v_ref[rows[u[1]], cols[u[0]]] for u in units}
    qk = {u: lax.dot_general(qb[u], k_ref[rows[u[1]], cols[u[0]]], (((1,), (1,)), ((), ())),
                             preferred_element_type=F32) for u in units}
    bcum = {u: jnp.sum(jnp.where(tril, frow_all[u[0]][:, rows[u[1]]], 0.0), axis=-1, keepdims=True)
            for u in units}
    bcum_row = {u: jnp.sum(jnp.where(r_i <= c_i, fcol_all[u[0]][rows[u[1]], :], 0.0), axis=0, keepdims=True)
                for u in units}
    dmat = {u: jnp.where(tril, bcum[u] - bcum_row[u] + irow_all[u[0]][:, rows[u[1]]], -jnp.inf) for u in units}
    mx = {u: jnp.max(dmat[u], axis=-1, keepdims=True) for u in units}
    b_last = {u: bcum[u][CHUNK - 1:CHUNK, :] for u in units}
    decay = {u: b_last[u] - bcum[u] + icol_all[u[0]][rows[u[1]], :] for u in units}
    mloc = {u: jnp.max(decay[u], axis=0, keepdims=True) for u in units}
    kwl = {u: k_ref[rows[u[1]], cols[u[0]]].astype(F32) * jnp.exp(decay[u] - mloc[u]) for u in units}
    upd = {u: lax.dot_general(kwl[u].astype(BF16), vb[u], (((0,), (0,)), ((), ())),
                              preferred_element_type=F32) for u in units}
    ksum = {u: jnp.sum(kwl[u], axis=0, keepdims=True) for u in units}
    sl = {u: qk[u] * jnp.exp(dmat[u] - mx[u]) for u in units}
    rs = {u: jnp.sum(sl[u], axis=-1, keepdims=True) for u in units}
    sv = {u: jnp.dot(sl[u].astype(BF16), vb[u], preferred_element_type=F32) for u in units}

    m_prev = [m_ref[h][:, 0:1] for h in hr]
    cmat = [c_ref[h] for h in hr]
    nvec = [n_ref[h] for h in hr]
    inter, q_state, q_n = {}, {}, {}
    for u in units:
        h = u[0]
        inter[u] = bcum[u] + m_prev[h]
        q_state[u] = jnp.dot(qb[u], cmat[h].astype(BF16), preferred_element_type=F32)
        q_n[u] = jnp.sum(qb[u].astype(F32) * nvec[h], axis=-1, keepdims=True)
        m_new = jnp.maximum(b_last[u] + m_prev[h], mloc[u])
        keep = jnp.exp(b_last[u] + m_prev[h] - m_new)
        add = jnp.exp(mloc[u] - m_new)
        cmat[h] = keep * cmat[h] + add * upd[u]
        nvec[h] = keep * nvec[h] + add * ksum[u]
        m_prev[h] = m_new
    for h in hr:
        c_ref[h] = cmat[h]
        n_ref[h] = nvec[h]
        m_ref[h] = jnp.broadcast_to(m_prev[h], m_ref.shape[1:])

    m_t = {u: jnp.maximum(inter[u], mx[u]) for u in units}
    r_intra = {u: jnp.exp(mx[u] - m_t[u]) for u in units}
    r_inter = {u: jnp.exp(inter[u] - m_t[u]) for u in units}
    inv = {u: 1.0 / jnp.maximum(jnp.abs(r_intra[u] * rs[u] + r_inter[u] * q_n[u]), jnp.exp(-m_t[u]))
           for u in units}
    w_sv = {u: r_intra[u] * inv[u] for u in units}
    w_state = {u: r_inter[u] * inv[u] for u in units}
    for u in units:
        hbuf[rows[u[1]], cols[u[0]]] = w_sv[u] * sv[u] + w_state[u] * q_state[u]

    for h in hr:
        hm = hbuf[:, cols[h]]
        hn = hm * lax.rsqrt(jnp.mean(hm * hm, axis=-1, keepdims=True) + EPS) * nw_ref[:, cols[h]]
        y_ref[:, cols[h]] = (hn * og_ref[:, cols[h]].astype(F32)).astype(BF16)


def _mlstm(proj, gcol, grow, conv_w, conv_b, mnw, batch, seq, ts, heads):
    nt = seq // ts
    hw = heads * M_HEAD_DIM

    def seg_spec(seg):
        return pl.BlockSpec((None, ts, hw), lambda b, h, t: (seg, b * nt + t, h))

    return pl.pallas_call(
        functools.partial(_mlstm_kernel, ts=ts, heads=heads),
        grid=(batch, M_HEADS // heads, nt),
        in_specs=[
            seg_spec(SEG_MQ), seg_spec(SEG_MK), seg_spec(SEG_MV), seg_spec(SEG_MG),
            pl.BlockSpec((ts, GATE_PAD), lambda b, h, t: (b * nt + t, 0)),
            pl.BlockSpec((8, ts), lambda b, h, t: (0, b * nt + t)),
            pl.BlockSpec((CONV_WIDTH, hw), lambda b, h, t: (0, h)),
            pl.BlockSpec((CONV_WIDTH, hw), lambda b, h, t: (0, M_HEADS // heads + h)),
            pl.BlockSpec((1, hw), lambda b, h, t: (0, h)),
            pl.BlockSpec((1, hw), lambda b, h, t: (0, M_HEADS // heads + h)),
            pl.BlockSpec((1, hw), lambda b, h, t: (0, h)),
        ],
        out_specs=pl.BlockSpec((ts, hw), lambda b, h, t: (b * nt + t, h)),
        out_shape=jax.ShapeDtypeStruct((batch * seq, M_WIDTH), BF16),
        scratch_shapes=[
            pltpu.VMEM((hw // 128, ts + CONV_HALO, 128), F32),
            pltpu.VMEM((hw // 128, ts + CONV_HALO, 128), F32),
            pltpu.VMEM((hw // 128, ts, 128), F32),
            pltpu.VMEM((hw // 128, ts, 128), F32),
            pltpu.VMEM((ts, hw), BF16),
            pltpu.VMEM((ts, hw), BF16),
            pltpu.VMEM((ts, hw), F32),
            pltpu.VMEM((heads, M_HEAD_DIM, M_HEAD_DIM), F32),
            pltpu.VMEM((heads, 1, M_HEAD_DIM), F32),
            pltpu.VMEM((heads, 1, 128), F32),
        ],
        compiler_params=pltpu.CompilerParams(
            dimension_semantics=("arbitrary", "arbitrary", "arbitrary"), vmem_limit_bytes=VMEM_LIMIT),
        name="mlstm",
    )(proj, proj, proj, proj, gcol, grow, conv_w, conv_w, conv_b, conv_b, mnw)


def _sb_kernel(q_ref, k_ref, v_ref, z_ref, qnw_ref, knw_ref, ntri_ref,
               y_ref, kn_ref, vt_ref, acc_ref, carry_ref, z2_ref, *, tq, heads):
    qi = pl.program_id(2)
    hd = SB_HEAD_DIM
    tk = tq
    hs = [slice(h * hd, (h + 1) * hd) for h in range(heads)]
    nt_dims = (((1,), (1,)), ((), ()))

    @pl.when(qi == 0)
    def _():
        for h in range(heads):
            kf = k_ref[:, hs[h]].astype(F32)
            kn = kf * lax.rsqrt(jnp.mean(kf * kf, axis=-1, keepdims=True) + EPS) * knw_ref[...]
            kn_ref[:, hs[h]] = kn.astype(BF16)
            for j in range(vt_ref.shape[0]):
                vt_ref[j, hs[h], :] = v_ref[j * tk:(j + 1) * tk, hs[h]].astype(F32).T.astype(BF16)

    qbs = []
    for h in range(heads):
        qf = q_ref[:, hs[h]].astype(F32)
        qn = qf * lax.rsqrt(jnp.mean(qf * qf, axis=-1, keepdims=True) + EPS) * qnw_ref[...]
        qbs.append((qn * (hd ** -0.5 * LOG2E)).astype(BF16))
    ntri = ntri_ref[...]

    hr = range(heads)

    def scores(kj):
        start = pl.multiple_of(kj * tk, tk)
        return [lax.dot_general(kn_ref[pl.ds(start, tk), hs[h]], qbs[h], nt_dims,
                                preferred_element_type=F32) for h in hr]

    def softplus2(z2):
        return jnp.maximum(z2, 0.0) + jnp.log(1.0 + jnp.exp2(-jnp.abs(z2))) * LOG2E

    def suffix(sp2):
        return jnp.dot(ntri, sp2.astype(BF16), preferred_element_type=F32)

    def causal_mask():
        s_i = lax.broadcasted_iota(jnp.int32, (tk, tq), 0)
        t_i = lax.broadcasted_iota(jnp.int32, (tk, tq), 1)
        return s_i < t_i

    def prefetch_scores(kj, slot):
        z2n = scores(jnp.maximum(kj, 0))
        for h in hr:
            z2_ref[slot, h] = z2n[h]

    @pl.when(qi == 0)
    def _():
        causal = causal_mask()
        z2d = scores(qi)
        csd = [suffix(jnp.where(causal, softplus2(z2), 0.0)) for z2 in z2d]
        avd = [jnp.where(causal, jnp.exp2(z2d[h] + csd[h]), 0.0).astype(BF16) for h in hr]
        for h in hr:
            acc_ref[hs[h], :] = jnp.dot(vt_ref[qi, hs[h], :], avd[h], preferred_element_type=F32)
            carry_ref[h:h + 1, :] = csd[h][0:1, :]

    @pl.when(qi > 0)
    def _():
        causal = causal_mask()
        z2d = scores(qi)
        z2p = scores(qi - 1)
        spd = [jnp.where(causal, softplus2(z2), 0.0) for z2 in z2d]
        spp = [softplus2(z2) for z2 in z2p]
        csd = [suffix(sp2) for sp2 in spd]
        csp = [suffix(sp2) for sp2 in spp]
        prefetch_scores(qi - 2, 0)
        avd = [jnp.where(causal, jnp.exp2(z2d[h] + csd[h]), 0.0).astype(BF16) for h in hr]
        avp = [jnp.exp2(z2p[h] + csp[h] + csd[h][0:1, :]).astype(BF16) for h in hr]
        for h in hr:
            acc_ref[hs[h], :] = (jnp.dot(vt_ref[qi, hs[h], :], avd[h], preferred_element_type=F32)
                                 + jnp.dot(vt_ref[qi - 1, hs[h], :], avp[h], preferred_element_type=F32))
            carry_ref[h:h + 1, :] = csd[h][0:1, :] + csp[h][0:1, :]

    def still_alive():
        return (jnp.max(carry_ref[0:heads, :]) >= SB_DEAD_LOG2).astype(jnp.int32)

    def cond(state):
        return jnp.logical_and(state[0] < qi, state[1] > 0)

    def body(state):
        i = state[0]
        slot = lax.rem(i - 1, 2)
        kj = qi - 1 - i
        z2s = [z2_ref[slot, h] for h in hr]
        css = [suffix(softplus2(z2)) for z2 in z2s]
        prefetch_scores(kj - 1, 1 - slot)
        carries = [carry_ref[h:h + 1, :] for h in hr]
        avs = [jnp.exp2(z2s[h] + css[h] + carries[h]).astype(BF16) for h in hr]
        for h in hr:
            acc_ref[hs[h], :] += jnp.dot(vt_ref[kj, hs[h], :], avs[h], preferred_element_type=F32)
            carry_ref[h:h + 1, :] = carries[h] + css[h][0:1, :]
        return i + 1, still_alive()

    lax.while_loop(cond, body, (jnp.int32(1), still_alive()))

    for h in range(heads):
        zg = z_ref[:, hs[h]].astype(F32)
        y_ref[:, hs[h]] = (acc_ref[hs[h], :].T * _silu(zg, 1.0)).astype(BF16)


def _stickbreak(proj, qnw, knw, ntri, batch, seq, tq, heads):
    nq = seq // tq
    hw = heads * SB_HEAD_DIM
    return pl.pallas_call(
        functools.partial(_sb_kernel, tq=tq, heads=heads),
        grid=(batch, SB_HEADS // heads, nq),
        in_specs=[
            pl.BlockSpec((None, tq, hw), lambda b, h, q: (SEG_SQ, b * nq + q, h)),
            pl.BlockSpec((None, seq, hw), lambda b, h, q: (SEG_SK, b, h)),
            pl.BlockSpec((None, seq, hw), lambda b, h, q: (SEG_SV, b, h)),
            pl.BlockSpec((None, tq, hw), lambda b, h, q: (SEG_SZ, b * nq + q, h)),
            pl.BlockSpec((1, SB_HEAD_DIM), lambda b, h, q: (0, 0)),
            pl.BlockSpec((1, SB_HEAD_DIM), lambda b, h, q: (0, 0)),
            pl.BlockSpec((tq, tq), lambda b, h, q: (0, 0)),
        ],
        out_specs=pl.BlockSpec((tq, hw), lambda b, h, q: (b * nq + q, h)),
        out_shape=jax.ShapeDtypeStruct((batch * seq, SB_WIDTH), BF16),
        scratch_shapes=[
            pltpu.VMEM((seq, hw), BF16),
            pltpu.VMEM((seq // tq, hw, tq), BF16),
            pltpu.VMEM((hw, tq), F32),
            pltpu.VMEM((8, tq), F32),
            pltpu.VMEM((2, heads, tq, tq), F32),
        ],
        compiler_params=pltpu.CompilerParams(
            dimension_semantics=("arbitrary", "arbitrary", "arbitrary"), vmem_limit_bytes=VMEM_LIMIT),
        name="stickbreak",
    )(proj, proj, proj, proj, qnw, knw, ntri)


def _merge_kernel(x_ref, ym_ref, ys_ref, gm_ref, gs_ref, wm_ref, ws_ref, wo_ref, out_ref):
    pm = jnp.dot(ym_ref[...], wm_ref[...], preferred_element_type=F32)
    ps = jnp.dot(ys_ref[...], ws_ref[...], preferred_element_type=F32)
    merged = _sigmoid(gm_ref[...].astype(F32)) * pm + _sigmoid(gs_ref[...].astype(F32)) * ps
    out_ref[...] = x_ref[...] + jnp.dot(merged.astype(BF16), wo_ref[...], preferred_element_type=F32)


def _merge(x2, y_m, y_s, proj, wm, ws, wo, tm):
    m_rows = x2.shape[0]
    row = lambda i: (i, 0)
    const = lambda i: (0, 0)
    return pl.pallas_call(
        _merge_kernel,
        grid=(m_rows // tm,),
        in_specs=[
            pl.BlockSpec((tm, D_MODEL), row),
            pl.BlockSpec((tm, M_WIDTH), row),
            pl.BlockSpec((tm, SB_WIDTH), row),
            pl.BlockSpec((None, tm, SEG_W), lambda i: (SEG_GM, i, 0)),
            pl.BlockSpec((None, tm, SEG_W), lambda i: (SEG_GS, i, 0)),
            pl.BlockSpec((M_WIDTH, D_MODEL), const),
            pl.BlockSpec((SB_WIDTH, D_MODEL), const),
            pl.BlockSpec((D_MODEL, D_MODEL), const),
        ],
        out_specs=pl.BlockSpec((tm, D_MODEL), row),
        out_shape=jax.ShapeDtypeStruct((m_rows, D_MODEL), F32),
        compiler_params=pltpu.CompilerParams(
            dimension_semantics=("arbitrary",), vmem_limit_bytes=VMEM_LIMIT),
        name="merge",
    )(x2, y_m, y_s, proj, proj, wm, ws, wo)


def _split_weights(w_in, b_in):
    mw = M_WIDTH
    g0 = 3 * mw
    g1 = g0 + 2 * M_HEADS
    w_main = jnp.concatenate([w_in[:, :g0], w_in[:, g1:]], axis=1)
    b_main = jnp.concatenate([b_in[:g0], b_in[g1:]])[None, :]
    w_gate = w_in[:, g0:g1]
    b_gate = b_in[g0:g1]
    wg = jnp.pad(w_gate, ((0, 0), (0, GATE_PAD - 2 * M_HEADS)))
    bg = jnp.pad(b_gate, (0, GATE_PAD - 2 * M_HEADS))[None, :]
    return w_main.astype(BF16), b_main, wg.astype(BF16), bg


def kernel(x, norm_w, w_in, b_in, conv_w, conv_b, mlstm_norm_w, sb_q_norm_w, sb_k_norm_w,
           w_proj_m, w_proj_s, w_out):
    batch, seq, d_model = x.shape
    assert d_model == D_MODEL and seq % 256 == 0
    m_rows = batch * seq
    x2 = x.reshape(m_rows, d_model)

    tm = min(1024, seq)
    ts = min(512, seq)
    tq = 256
    assert seq % tm == 0 and seq % ts == 0

    w_main, b_main, wg, bg = _split_weights(w_in, b_in)
    proj, gcol, grow = _inproj(x2, norm_w[None, :], w_main, b_main, wg, bg, tm)

    y_m = _mlstm(proj, gcol, grow, conv_w, conv_b[None, :], mlstm_norm_w[None, :], batch, seq, ts, M_GROUP)

    ntri = -(jnp.arange(tq)[:, None] <= jnp.arange(tq)[None, :]).astype(BF16)
    y_s = _stickbreak(proj, sb_q_norm_w[None, :], sb_k_norm_w[None, :], ntri, batch, seq, tq, SB_GROUP)

    out = _merge(x2, y_m, y_s, proj, w_proj_m.astype(BF16), w_proj_s.astype(BF16),
                 w_out.astype(BF16), tm)
    return out.reshape(batch, seq, d_model)
```
